```python
import math, functools
import jax, jax.numpy as jnp
from jax import lax
import numpy as np

D_MODEL = 1024
BATCH = 32
SEQ = 2048
DEPTH = 1
DEC_BATCH = 128
DEC_SEQ = 1
PAST_LEN = 8192
PAGE_SIZE = 128

MIX_WIDTH = D_MODEL
HG_WIDTH = MIX_WIDTH // 2
HG_HEADS = 4
HG_KEY = HG_WIDTH // HG_HEADS
HG_VAL = HG_WIDTH // HG_HEADS
HG_CHUNK = 64
ATT_WIDTH = MIX_WIDTH - HG_WIDTH
ATT_HEADS = 8
ATT_HEAD_DIM = ATT_WIDTH // ATT_HEADS
MOBA_BLOCK = 256
MOBA_TOPK = 3
MOBA_Q_CHUNK = 8
ROPE_THETA = 500000.0
ROPE_DIMS = ATT_HEAD_DIM // 4
D_FF = 2816
CONV_WIDTH = 3
NORM_EPS = 1e-6
IN_COLS = 4 * HG_WIDTH + 3 * ATT_WIDTH
SPLITS = (HG_WIDTH, 2 * HG_WIDTH, 3 * HG_WIDTH, 4 * HG_WIDTH,
          4 * HG_WIDTH + ATT_WIDTH, 4 * HG_WIDTH + 2 * ATT_WIDTH)

kernel_name = 'hymba_hgrn2_moba_convffn_step'

F32 = jnp.float32


def rmsnorm(x, g):
    xf = x.astype(F32)
    y = xf * lax.rsqrt(jnp.mean(xf * xf, axis=-1, keepdims=True) + NORM_EPS)
    return (y * g.astype(F32)).astype(x.dtype)


def partial_rope(x, pos):
    half = ROPE_DIMS // 2
    inv = jnp.power(jnp.float32(ROPE_THETA), -jnp.arange(half, dtype=F32) * (2.0 / ROPE_DIMS))
    ang = pos.astype(F32)[:, None] * inv[None, :]
    cos = jnp.cos(ang)[:, None, :]
    sin = jnp.sin(ang)[:, None, :]
    xf = x.astype(F32)
    x1 = xf[..., :half]
    x2 = xf[..., half:ROPE_DIMS]
    out = jnp.concatenate([x1 * cos - x2 * sin, x2 * cos + x1 * sin, xf[..., ROPE_DIMS:]], axis=-1)
    return out.astype(x.dtype)


def hgrn_recurrence(q, logf, k, v, s0):
    B, H, L, _ = q.shape
    c = math.gcd(L, HG_CHUNK)
    n = L // c

    def to_chunks(a):
        return a.reshape(B, H, n, c, a.shape[-1]).transpose(2, 0, 1, 3, 4)

    causal = jnp.tril(jnp.ones((c, c), dtype=bool))

    def step(S, inp):
        qc, lfc, kc, vc = inp
        b = jnp.cumsum(lfc, axis=2)
        o_inter = jnp.einsum('bhtk,bhkv->bhtv', qc * jnp.exp(b), S)
        diff = b[:, :, :, None, :] - b[:, :, None, :, :]
        decay = jnp.exp(jnp.where(causal[:, :, None], diff, -jnp.inf))
        A = jnp.einsum('bhtk,bhtsk,bhsk->bhts', qc, decay, kc)
        o = o_inter + jnp.einsum('bhts,bhsv->bhtv', A, vc)
        b_end = b[:, :, -1:, :]
        S = jnp.exp(b_end[:, :, 0, :])[..., None] * S + jnp.einsum(
            'bhsk,bhsv->bhkv', kc * jnp.exp(b_end - b), vc)
        return S, o

    S, o = lax.scan(step, s0, (to_chunks(q), to_chunks(logf), to_chunks(k), to_chunks(v)))
    o = o.transpose(1, 2, 0, 3, 4).reshape(B, H, L, o.shape[-1])
    return o, S


def hgrn_mixer(hq, hf, hi, hg, lb, norm_g, s0):
    B, L, _ = hq.shape
    q = jax.nn.silu(hq.astype(F32))
    f = lb + (1.0 - lb) * jax.nn.sigmoid(hf.astype(F32))
    logf = jnp.log(f)
    k = 1.0 - f
    v = hi.astype(F32)

    def heads(a):
        return a.reshape(B, L, HG_HEADS, -1).transpose(0, 2, 1, 3)

    o, S = hgrn_recurrence(heads(q), heads(logf), heads(k), heads(v), s0.astype(F32))
    o = rmsnorm(o.transpose(0, 2, 1, 3), norm_g)
    o = o * jax.nn.silu(hg.astype(F32)).reshape(B, L, HG_HEADS, HG_VAL)
    return o.reshape(B, L, HG_WIDTH).astype(hq.dtype), S


def moba_select(q, q_pos, kmean):
    nb = kmean.shape[1]
    kk = min(MOBA_TOPK, nb)
    gate = jnp.einsum('bhqd,bnhd->bhqn', q.astype(F32), kmean)
    own_blk = (q_pos // MOBA_BLOCK).astype(jnp.int32)
    fully_past = jnp.arange(nb)[None, :] < own_blk[:, None]
    vals, idx = lax.top_k(jnp.where(fully_past, gate, -jnp.inf), kk)
    own = jnp.broadcast_to(own_blk[None, None, :, None], idx.shape[:-1] + (1,))
    blk = jnp.concatenate([idx.astype(jnp.int32), own], axis=-1)
    ok = jnp.concatenate([jnp.isfinite(vals), jnp.ones(own.shape, dtype=bool)], axis=-1)
    return blk, ok


def block_positions(blk):
    return blk[..., None] * MOBA_BLOCK + jnp.arange(MOBA_BLOCK, dtype=jnp.int32)


def gather_rows(rows, pos):
    B, _, H, _ = rows.shape
    bi = jnp.arange(B)[:, None, None, None, None]
    hi = jnp.arange(H)[None, :, None, None, None]
    return rows[bi, pos, hi]


def paged_rows(pool, layer, new, page_table, pos):
    B, n_pages = page_table.shape
    past = n_pages * PAGE_SIZE
    T = new.shape[1]
    bi = jnp.arange(B)[:, None, None, None, None]
    hi = jnp.arange(new.shape[2])[None, :, None, None, None]
    pc = jnp.minimum(pos, past - 1)
    phys = page_table[bi, pc // PAGE_SIZE]
    from_pool = pool[layer, phys, pc % PAGE_SIZE, hi]
    from_new = new[bi, jnp.clip(pos - past, 0, T - 1), hi].astype(pool.dtype)
    return jnp.where((pos < past)[..., None], from_pool, from_new)


def moba_core(q, q_pos, pos, ok, k_sel, v_sel):
    mask = ok[..., None] & (pos <= q_pos[None, None, :, None, None])
    s = jnp.einsum('bhqd,bhqjkd->bhqjk', q, k_sel, preferred_element_type=F32) * (ATT_HEAD_DIM ** -0.5)
    s = jnp.where(mask, s, -jnp.inf)
    B, H, Q, J, K = s.shape
    p = jax.nn.softmax(s.reshape(B, H, Q, J * K), axis=-1).reshape(s.shape)
    return jnp.einsum('bhqjk,bhqjkd->bhqd', p.astype(v_sel.dtype), v_sel)


def moba_prompt(aq, ak, av):
    B, S, _ = aq.shape
    pos = jnp.arange(S, dtype=jnp.int32)
    q = partial_rope(aq.reshape(B, S, ATT_HEADS, ATT_HEAD_DIM), pos)
    k = partial_rope(ak.reshape(B, S, ATT_HEADS, ATT_HEAD_DIM), pos)
    v = av.reshape(B, S, ATT_HEADS, ATT_HEAD_DIM)
    nb = -(-S // MOBA_BLOCK)
    pad = ((0, 0), (0, nb * MOBA_BLOCK - S), (0, 0), (0, 0))
    kp = jnp.pad(k, pad)
    vp = jnp.pad(v, pad)
    kmean = jnp.mean(kp.reshape(B, nb, MOBA_BLOCK, ATT_HEADS, ATT_HEAD_DIM), axis=2, dtype=F32)
    qh = q.transpose(0, 2, 1, 3)
    blk, ok = moba_select(qh, pos, kmean)
    J = blk.shape[-1]
    nq = S // MOBA_Q_CHUNK
    qc = qh.reshape(B, ATT_HEADS, nq, MOBA_Q_CHUNK, ATT_HEAD_DIM).transpose(2, 0, 1, 3, 4)
    blk_c = blk.reshape(B, ATT_HEADS, nq, MOBA_Q_CHUNK, J).transpose(2, 0, 1, 3, 4)
    ok_c = ok.reshape(B, ATT_HEADS, nq, MOBA_Q_CHUNK, J).transpose(2, 0, 1, 3, 4)
    pos_c = pos.reshape(nq, MOBA_Q_CHUNK)

    def attend(inp):
        qq, bb, oo, pp = inp
        kpos = block_positions(bb)
        return moba_core(qq, pp, kpos, oo, gather_rows(kp, kpos), gather_rows(vp, kpos))

    o = lax.map(attend, (qc, blk_c, ok_c, pos_c))
    o = o.transpose(1, 0, 3, 2, 4).reshape(B, S, ATT_WIDTH)
    return o, k, v


def moba_sample(aq, ak, av, cache_k, cache_v, page_table, layer):
    B, T, _ = aq.shape
    past = page_table.shape[1] * PAGE_SIZE
    pos = past + jnp.arange(T, dtype=jnp.int32)
    q = partial_rope(aq.reshape(B, T, ATT_HEADS, ATT_HEAD_DIM), pos)
    k = partial_rope(ak.reshape(B, T, ATT_HEADS, ATT_HEAD_DIM), pos)
    v = av.reshape(B, T, ATT_HEADS, ATT_HEAD_DIM)
    k_past = cache_k[layer, page_table].reshape(B, past, ATT_HEADS, ATT_HEAD_DIM)
    L = past + T
    nb = -(-L // MOBA_BLOCK)
    k_all = jnp.pad(jnp.concatenate([k_past, k.astype(k_past.dtype)], axis=1),
                    ((0, 0), (0, nb * MOBA_BLOCK - L), (0, 0), (0, 0)))
    kmean = jnp.mean(k_all.reshape(B, nb, MOBA_BLOCK, ATT_HEADS, ATT_HEAD_DIM), axis=2, dtype=F32)
    qh = q.transpose(0, 2, 1, 3)
    blk, ok = moba_select(qh, pos, kmean)
    kpos = block_positions(blk)
    k_sel = gather_rows(k_all, kpos)
    v_sel = paged_rows(cache_v, layer, v, page_table, kpos)
    o = moba_core(qh.astype(k_sel.dtype), pos, kpos, ok, k_sel, v_sel)
    return o.transpose(0, 2, 1, 3).reshape(B, T, ATT_WIDTH), k, v


def conv_ffn(h, prev, w_in, conv_w, conv_b, w_out):
    u = h @ w_in
    gate, up = jnp.split(u, 2, axis=-1)
    L = gate.shape[1]
    ext = jnp.concatenate([prev.astype(gate.dtype), gate], axis=1)
    conv = sum(ext[:, j:j + L] * conv_w[j] for j in range(CONV_WIDTH)) + conv_b
    a = jax.nn.gelu(conv, approximate=True) * up
    return a @ w_out, ext[:, -(CONV_WIDTH - 1):]


def layer_step(x, attend, s_hg, conv_prev, w_in, w_out, lb, hg_norm, g_pre_mix, g_post_mix,
               g_pre_ffn, g_post_ffn, w_ffn_in, conv_w, conv_b, w_ffn_out):
    h = rmsnorm(x, g_pre_mix)
    z = h @ w_in
    hq, hf, hi, hg, aq, ak, av = jnp.split(z, SPLITS, axis=-1)
    o_hg, s_new = hgrn_mixer(hq, hf, hi, hg, lb, hg_norm, s_hg)
    o_att, k_new, v_new = attend(aq, ak, av)
    mix = jnp.concatenate([o_hg, o_att.astype(o_hg.dtype)], axis=-1) @ w_out
    x = x + rmsnorm(mix, g_post_mix)
    f, conv_new = conv_ffn(rmsnorm(x, g_pre_ffn), conv_prev, w_ffn_in, conv_w, conv_b, w_ffn_out)
    x = x + rmsnorm(f, g_post_ffn)
    return x, k_new, v_new, s_new, conv_new


def setup_inputs(seed: int = 0) -> dict:
    key = jax.random.key(seed)
    ks = jax.random.split(key, 24)
    n_pages = PAST_LEN // PAGE_SIZE
    n_pool = (5 * DEC_BATCH * n_pages) // 4
    nrm = jax.random.normal
    page_table = jax.random.permutation(ks[0], n_pool)[:DEC_BATCH * n_pages].reshape(
        DEC_BATCH, n_pages).astype(jnp.int32)
    return {
        'x_prompt': nrm(ks[1], (BATCH, SEQ, D_MODEL), F32),
        'x_sample': nrm(ks[2], (DEC_BATCH, DEC_SEQ, D_MODEL), F32),
        'cache_k': nrm(ks[3], (DEPTH, n_pool, PAGE_SIZE, ATT_HEADS, ATT_HEAD_DIM), F32),
        'cache_v': nrm(ks[4], (DEPTH, n_pool, PAGE_SIZE, ATT_HEADS, ATT_HEAD_DIM), F32),
        'state_hgrn': 0.5 * nrm(ks[5], (DEPTH, DEC_BATCH, HG_HEADS, HG_KEY, HG_VAL), F32),
        'state_conv': nrm(ks[6], (DEPTH, DEC_BATCH, CONV_WIDTH - 1, D_FF), F32),
        'page_table': page_table,
        'w_in': nrm(ks[7], (DEPTH, D_MODEL, IN_COLS), F32) * D_MODEL ** -0.5,
        'w_out': nrm(ks[8], (DEPTH, MIX_WIDTH, D_MODEL), F32) * MIX_WIDTH ** -0.5,
        'hg_lb_logits': 0.5 * nrm(ks[9], (DEPTH + 1, HG_WIDTH), F32),
        'hg_norm': 1.0 + 0.02 * nrm(ks[10], (DEPTH, HG_VAL), F32),
        'g_pre_mix': 1.0 + 0.02 * nrm(ks[11], (DEPTH, D_MODEL), F32),
        'g_post_mix': 1.0 + 0.02 * nrm(ks[12], (DEPTH, D_MODEL), F32),
        'g_pre_ffn': 1.0 + 0.02 * nrm(ks[13], (DEPTH, D_MODEL), F32),
        'g_post_ffn': 1.0 + 0.02 * nrm(ks[14], (DEPTH, D_MODEL), F32),
        'w_ffn_in': nrm(ks[15], (DEPTH, D_MODEL, 2 * D_FF), F32) * D_MODEL ** -0.5,
        'conv_w': nrm(ks[16], (DEPTH, CONV_WIDTH, D_FF), F32) * CONV_WIDTH ** -0.5,
        'conv_b': 0.02 * nrm(ks[17], (DEPTH, D_FF), F32),
        'w_ffn_out': nrm(ks[18], (DEPTH, D_FF, D_MODEL), F32) * D_FF ** -0.5,
    }


def reference(x_prompt, x_sample, cache_k, cache_v, state_hgrn, state_conv, page_table,
              w_in, w_out, hg_lb_logits, hg_norm, g_pre_mix, g_post_mix, g_pre_ffn, g_post_ffn,
              w_ffn_in, conv_w, conv_b, w_ffn_out):
    lb_all = jnp.cumsum(jax.nn.softmax(hg_lb_logits.astype(F32), axis=0), axis=0)
    xp, xs = x_prompt, x_sample
    kp_l, vp_l, sp_l, cp_l = [], [], [], []
    ks_l, vs_l, ss_l, cs_l = [], [], [], []
    for l in range(DEPTH):
        w = (w_in[l], w_out[l], lb_all[l], hg_norm[l], g_pre_mix[l], g_post_mix[l],
             g_pre_ffn[l], g_post_ffn[l], w_ffn_in[l], conv_w[l], conv_b[l], w_ffn_out[l])
        s0 = jnp.zeros((xp.shape[0], HG_HEADS, HG_KEY, HG_VAL), F32)
        c0 = jnp.zeros((xp.shape[0], CONV_WIDTH - 1, D_FF), xp.dtype)
        xp, k_n, v_n, s_n, c_n = layer_step(xp, moba_prompt, s0, c0, *w)
        kp_l.append(k_n); vp_l.append(v_n)
        sp_l.append(s_n.astype(state_hgrn.dtype)); cp_l.append(c_n.astype(state_conv.dtype))
        attend_s = functools.partial(moba_sample, cache_k=cache_k, cache_v=cache_v,
                                     page_table=page_table, layer=l)
        xs, k_n, v_n, s_n, c_n = layer_step(xs, attend_s, state_hgrn[l], state_conv[l], *w)
        ks_l.append(k_n); vs_l.append(v_n)
        ss_l.append(s_n.astype(state_hgrn.dtype)); cs_l.append(c_n.astype(state_conv.dtype))
    return (xp, xs,
            jnp.stack(kp_l), jnp.stack(vp_l), jnp.stack(sp_l), jnp.stack(cp_l),
            jnp.stack(ks_l), jnp.stack(vs_l), jnp.stack(ss_l), jnp.stack(cs_l))
```

```python
import functools

import jax
import jax.numpy as jnp
from jax import lax
from jax.experimental import pallas as pl
from jax.experimental.pallas import tpu as pltpu

F32 = jnp.float32
BF16 = jnp.bfloat16
I32 = jnp.int32

HG_HEADS = 4
HG_DIM = 128
HG_WIDTH = HG_HEADS * HG_DIM
ATT_HEADS = 8
ATT_DIM = 64
ATT_WIDTH = ATT_HEADS * ATT_DIM
MOBA_BLOCK = 256
MOBA_TOPK = 3
ROPE_THETA = 500000.0
ROPE_DIMS = ATT_DIM // 4
ROPE_HALF = ROPE_DIMS // 2
CONV_WIDTH = 3
NORM_EPS = 1e-6
PAGE_SIZE = 128
PAGES_PER_BLOCK = MOBA_BLOCK // PAGE_SIZE

LANES = 128
SUBLANES = 8
VMEM_LIMIT = 56 * 1024 * 1024

NEG_INF = float("-inf")
_NT = (((1,), (1,)), ((), ()))
_TN = (((0,), (0,)), ((), ()))


def _dot(a, b):
    return jnp.dot(a, b, preferred_element_type=F32)


def _rms(x, g):
    return x * lax.rsqrt(jnp.mean(x * x, axis=-1, keepdims=True) + NORM_EPS) * g


def _sigmoid(x):
    return 1.0 / (1.0 + jnp.exp(-x))


def _split3(x):
    x1 = x.astype(BF16)
    r1 = x - x1.astype(F32)
    x2 = r1.astype(BF16)
    r2 = r1 - x2.astype(F32)
    return x1, x2, r2.astype(BF16)


def _dot_f32(a, b, dims):
    a1, a2, a3 = _split3(a)
    b1, b2, b3 = _split3(b)
    dg = functools.partial(lax.dot_general, dimension_numbers=dims, preferred_element_type=F32)
    small = dg(a1, b3) + dg(a3, b1) + dg(a2, b2)
    mid = dg(a1, b2) + dg(a2, b1)
    return dg(a1, b1) + (mid + small)


def _inproj_kernel(x_ref, g_ref, w_ref, cos_ref, sa_ref, sb_ref, zh_ref, q_ref, k_ref, v_ref):
    h = _rms(x_ref[...], g_ref[...]).astype(BF16)
    hgw = 4 * HG_WIDTH
    for c in range(4):
        zh_ref[:, c * HG_WIDTH:(c + 1) * HG_WIDTH] = _dot(h, w_ref[:, c * HG_WIDTH:(c + 1) * HG_WIDTH])
    cos = cos_ref[...]
    sa = sa_ref[...]
    sb = sb_ref[...]
    for out_ref, off in ((q_ref, hgw), (k_ref, hgw + ATT_WIDTH)):
        a = _dot(h, w_ref[:, off:off + ATT_WIDTH])
        for j in range(ATT_WIDTH // LANES):
            aj = a[:, j * LANES:(j + 1) * LANES]
            out_ref[:, j * LANES:(j + 1) * LANES] = (
                aj * cos + pltpu.roll(aj, LANES - ROPE_HALF, 1) * sa + pltpu.roll(aj, ROPE_HALF, 1) * sb)
    v_ref[...] = _dot(h, w_ref[:, hgw + 2 * ATT_WIDTH:hgw + 3 * ATT_WIDTH])


def _rope_tables(pos):
    inv = jnp.power(jnp.float32(ROPE_THETA), -jnp.arange(ROPE_HALF, dtype=F32) * (2.0 / ROPE_DIMS))
    ang = pos.astype(F32)[:, None] * inv[None, :]
    cos = jnp.cos(ang)
    sin = jnp.sin(ang)
    n = pos.shape[0]
    rest = ATT_DIM - ROPE_DIMS
    cos_h = jnp.concatenate([cos, cos, jnp.ones((n, rest), F32)], axis=1)
    sa_h = jnp.concatenate([-sin, jnp.zeros((n, ATT_DIM - ROPE_HALF), F32)], axis=1)
    sb_h = jnp.concatenate([jnp.zeros((n, ROPE_HALF), F32), sin, jnp.zeros((n, rest), F32)], axis=1)
    rep = LANES // ATT_DIM
    return tuple(jnp.tile(t, (1, rep)) for t in (cos_h, sa_h, sb_h))


def _inproj(x2d, g, w_bf16, tables, tm):
    rows, d = x2d.shape
    ncols = w_bf16.shape[1]
    npos = tables[0].shape[0] // tm
    tab_spec = pl.BlockSpec((tm, LANES), lambda i: (i % npos, 0))
    row_spec = lambda width: pl.BlockSpec((tm, width), lambda i: (i, 0))
    return pl.pallas_call(
        _inproj_kernel,
        grid=(rows // tm,),
        in_specs=[row_spec(d),
                  pl.BlockSpec((1, d), lambda i: (0, 0)),
                  pl.BlockSpec((d, ncols), lambda i: (0, 0), pipeline_mode=pl.Buffered(1)),
                  tab_spec, tab_spec, tab_spec],
        out_specs=[row_spec(4 * HG_WIDTH), row_spec(ATT_WIDTH), row_spec(ATT_WIDTH), row_spec(ATT_WIDTH)],
        out_shape=[jax.ShapeDtypeStruct((rows, 4 * HG_WIDTH), F32),
                   jax.ShapeDtypeStruct((rows, ATT_WIDTH), F32),
                   jax.ShapeDtypeStruct((rows, ATT_WIDTH), F32),
                   jax.ShapeDtypeStruct((rows, ATT_WIDTH), F32)],
        compiler_params=pltpu.CompilerParams(dimension_semantics=("arbitrary",), vmem_limit_bytes=VMEM_LIMIT),
        name="inproj",
    )(x2d, g, w_bf16, *tables)


def _hgrn_gates(zh_row, lb, h):
    sl = lambda part: slice(part * HG_WIDTH + h * HG_DIM, part * HG_WIDTH + (h + 1) * HG_DIM)
    hq = zh_row[:, sl(0)]
    hf = zh_row[:, sl(1)]
    v = zh_row[:, sl(2)]
    hg = zh_row[:, sl(3)]
    q = hq * _sigmoid(hq)
    f = lb + (1.0 - lb) * _sigmoid(hf)
    return q, f, 1.0 - f, v, hg * _sigmoid(hg)


def _level_reference(b, b3, sub, level, c):
    m = 1 << level
    if m >= SUBLANES:
        pieces = []
        for grp in range(c // (2 * m)):
            r = grp * 2 * m + m - 1
            pieces.append(jnp.broadcast_to(b[r:r + 1, :], (2 * m, HG_DIM)))
        return pieces[0] if len(pieces) == 1 else jnp.concatenate(pieces, axis=0)
    beta3 = None
    for g0 in range(0, SUBLANES, 2 * m):
        r = g0 + m - 1
        piece = jnp.broadcast_to(b3[:, r:r + 1, :], b3.shape)
        beta3 = piece if beta3 is None else jnp.where(sub >= g0, piece, beta3)
    return beta3.reshape(c, HG_DIM)


def _hgrn_prompt_kernel(zh_ref, lb_ref, ng_ref, o_ref, s_out_ref, st_ref, lv_ref, *, chunk):
    c = chunk
    ci = pl.program_id(1)
    nlev = c.bit_length() - 1
    rows = lax.broadcasted_iota(I32, (c, c), 0)
    cols = lax.broadcasted_iota(I32, (c, c), 1)

    @pl.when(ci == 0)
    def _():
        st_ref[...] = jnp.zeros_like(st_ref)
        x = rows ^ cols
        lev = jnp.zeros((c, c), I32)
        for j in range(1, nlev):
            lev = lev + jnp.where(x >= (1 << j), 1, 0)
        lv_ref[...] = jnp.where(cols < rows, lev, jnp.where(cols == rows, -1, -2))

    lv = lv_ref[...]
    tri = jnp.where(cols <= rows, 1.0, 0.0).astype(BF16)
    sub = lax.broadcasted_iota(I32, (c // SUBLANES, SUBLANES, HG_DIM), 1)
    zh = zh_ref[0]
    ng = ng_ref[...]
    for h in range(HG_HEADS):
        lb = lb_ref[:, h * HG_DIM:(h + 1) * HG_DIM]
        q, f, k, v, og = _hgrn_gates(zh, lb, h)
        g1, g2, g3 = _split3(jnp.log(f))
        b = _dot(tri, g1) + (_dot(tri, g2) + _dot(tri, g3))
        b3 = b.reshape(c // SUBLANES, SUBLANES, HG_DIM)
        vb = v.astype(BF16)

        a = jnp.where(lv == -1, jnp.sum(q * k, axis=-1, keepdims=True), 0.0)
        for level in range(nlev):
            w = jnp.exp(-jnp.abs(b - _level_reference(b, b3, sub, level, c)))
            a_l = lax.dot_general((q * w).astype(BF16), (k * w).astype(BF16), _NT, preferred_element_type=F32)
            a = jnp.where(lv == level, a_l, a)

        st = st_ref[h]
        o = _dot(a.astype(BF16), vb) + lax.dot_general(
            (q * jnp.exp(b)).astype(BF16), st.astype(BF16), _NT, preferred_element_type=F32)
        b_end = b[c - 1:c, :]
        k_end = (k * jnp.exp(b_end - b)).astype(BF16)
        st_new = st * jnp.exp(b_end) + lax.dot_general(vb, k_end, _TN, preferred_element_type=F32)
        st_ref[h] = st_new
        o_ref[0, :, h * HG_DIM:(h + 1) * HG_DIM] = _rms(o, ng) * og

        @pl.when(ci == pl.num_programs(1) - 1)
        def _():
            s_out_ref[0, h] = st_new.T


def _hgrn_prompt(zh3, lb, ng, chunk):
    b, l, _ = zh3.shape
    return pl.pallas_call(
        functools.partial(_hgrn_prompt_kernel, chunk=chunk),
        grid=(b, l // chunk),
        in_specs=[pl.BlockSpec((1, chunk, 4 * HG_WIDTH), lambda i, j: (i, j, 0)),
                  pl.BlockSpec((1, HG_WIDTH), lambda i, j: (0, 0)),
                  pl.BlockSpec((1, HG_DIM), lambda i, j: (0, 0))],
        out_specs=[pl.BlockSpec((1, chunk, HG_WIDTH), lambda i, j: (i, j, 0)),
                   pl.BlockSpec((1, HG_HEADS, HG_DIM, HG_DIM), lambda i, j: (i, 0, 0, 0))],
        out_shape=[jax.ShapeDtypeStruct((b, l, HG_WIDTH), F32),
                   jax.ShapeDtypeStruct((b, HG_HEADS, HG_DIM, HG_DIM), F32)],
        scratch_shapes=[pltpu.VMEM((HG_HEADS, HG_DIM, HG_DIM), F32), pltpu.VMEM((chunk, chunk), I32)],
        compiler_params=pltpu.CompilerParams(dimension_semantics=("arbitrary", "arbitrary"),
                                             vmem_limit_bytes=VMEM_LIMIT),
        name="hgrn_prompt",
    )(zh3, lb, ng)


def _hgrn_step_kernel(zh_ref, lb_ref, ng_ref, s_ref, o_ref, s_out_ref):
    zh = zh_ref[0]
    ng = ng_ref[...]
    eye = (lax.broadcasted_iota(I32, (HG_DIM, HG_DIM), 0) == lax.broadcasted_iota(I32, (HG_DIM, HG_DIM), 1))

    def column(row):
        return jnp.sum(jnp.where(eye, row, 0.0), axis=1, keepdims=True)

    for h in range(HG_HEADS):
        lb = lb_ref[:, h * HG_DIM:(h + 1) * HG_DIM]
        q, f, k, v, og = _hgrn_gates(zh, lb, h)
        s_new = column(f) * s_ref[0, h] + column(k) * v
        s_out_ref[0, h] = s_new
        o = jnp.sum(column(q) * s_new, axis=0, keepdims=True)
        o_ref[0, :, h * HG_DIM:(h + 1) * HG_DIM] = _rms(o, ng) * og


def _hgrn_step(zh3, lb, ng, state):
    b = zh3.shape[0]
    return pl.pallas_call(
        _hgrn_step_kernel,
        grid=(b,),
        in_specs=[pl.BlockSpec((1, 1, 4 * HG_WIDTH), lambda i: (i, 0, 0)),
                  pl.BlockSpec((1, HG_WIDTH), lambda i: (0, 0)),
                  pl.BlockSpec((1, HG_DIM), lambda i: (0, 0)),
                  pl.BlockSpec((1, HG_HEADS, HG_DIM, HG_DIM), lambda i: (i, 0, 0, 0))],
        out_specs=[pl.BlockSpec((1, 1, HG_WIDTH), lambda i: (i, 0, 0)),
                   pl.BlockSpec((1, HG_HEADS, HG_DIM, HG_DIM), lambda i: (i, 0, 0, 0))],
        out_shape=[jax.ShapeDtypeStruct((b, 1, HG_WIDTH), F32),
                   jax.ShapeDtypeStruct(state.shape, F32)],
        compiler_params=pltpu.CompilerParams(dimension_semantics=("arbitrary",)),
        name="hgrn_step",
    )(zh3, lb, ng, state)


def _topk_rank(gm, axis, n_candidates, idx):
    rank = jnp.zeros(gm.shape, F32)
    for m in range(n_candidates):
        other = lax.slice_in_dim(gm, m, m + 1, axis=axis)
        tie = jnp.where(other == gm, jnp.where(idx > m, 1.0, 0.0), 0.0)
        rank = rank + jnp.where(other > gm, 1.0, tie)
    return rank


def _moba_prompt_kernel(q_ref, k_ref, v_ref, o_ref):
    l = q_ref.shape[1]
    nb = l // MOBA_BLOCK
    scale = ATT_DIM ** -0.5
    blk_idx = lax.broadcasted_iota(I32, (l, LANES), 1)
    own_blk = lax.broadcasted_iota(I32, (l, LANES), 0) // MOBA_BLOCK
    causal = (lax.broadcasted_iota(I32, (MOBA_BLOCK, MOBA_BLOCK), 1)
              <= lax.broadcasted_iota(I32, (MOBA_BLOCK, MOBA_BLOCK), 0))
    for hh in range(LANES // ATT_DIM):
        lanes = slice(hh * ATT_DIM, (hh + 1) * ATT_DIM)
        qh = q_ref[0, :, lanes]
        kh = k_ref[0, :, lanes]
        vb = v_ref[0, :, lanes].astype(BF16)
        kmean = jnp.sum(kh.reshape(nb, MOBA_BLOCK, ATT_DIM), axis=1) * (1.0 / MOBA_BLOCK)
        kmean = jnp.concatenate([kmean, jnp.zeros((LANES - nb, ATT_DIM), F32)], axis=0)
        gate = _dot_f32(qh, kmean, _NT)
        valid = blk_idx < own_blk
        rank = _topk_rank(jnp.where(valid, gate, NEG_INF), 1, nb, blk_idx)
        sel = jnp.where(valid, jnp.where(rank < MOBA_TOPK, 1.0, 0.0), 0.0)
        qb = qh.astype(BF16)
        kb = kh.astype(BF16)
        for i in range(nb):
            rows = slice(i * MOBA_BLOCK, (i + 1) * MOBA_BLOCK)
            n_keys = (i + 1) * MOBA_BLOCK
            s = lax.dot_general(qb[rows], kb[:n_keys], _NT, preferred_element_type=F32) * scale
            pieces = [jnp.where(sel[rows, j:j + 1] > 0.5, s[:, j * MOBA_BLOCK:(j + 1) * MOBA_BLOCK], NEG_INF)
                      for j in range(i)]
            pieces.append(jnp.where(causal, s[:, i * MOBA_BLOCK:], NEG_INF))
            sm = pieces[0] if i == 0 else jnp.concatenate(pieces, axis=1)
            p = jnp.exp(sm - jnp.max(sm, axis=1, keepdims=True))
            den = jnp.sum(p, axis=1, keepdims=True)
            o_ref[0, rows, lanes] = _dot(p.astype(BF16), vb[:n_keys]) / den


def _moba_prompt(q3, k3, v3):
    b, l, _ = q3.shape
    spec = pl.BlockSpec((1, l, LANES), lambda i, j: (i, 0, j))
    return pl.pallas_call(
        _moba_prompt_kernel,
        grid=(b, ATT_WIDTH // LANES),
        in_specs=[spec, spec, spec],
        out_specs=spec,
        out_shape=jax.ShapeDtypeStruct((b, l, ATT_WIDTH), F32),
        compiler_params=pltpu.CompilerParams(dimension_semantics=("arbitrary", "arbitrary"),
                                             vmem_limit_bytes=VMEM_LIMIT),
        name="moba_prompt",
    )(q3, k3, v3)


SEL_PAGES = 8


def _moba_select_kernel(pt_ref, q_ref, kc_ref, sel_ref, buf_ref, ksum_ref, sem_ref, *, n_pages):
    bi = pl.program_id(0)
    ci = pl.program_id(1)
    nch = pl.num_programs(1)
    step = bi * nch + ci
    slot = step % 2
    nblk = n_pages // PAGES_PER_BLOCK

    def copies(b_idx, c_idx, slot_idx):
        return [pltpu.make_async_copy(kc_ref.at[0, pt_ref[b_idx, c_idx * SEL_PAGES + p]],
                                      buf_ref.at[slot_idx, p], sem_ref.at[slot_idx])
                for p in range(SEL_PAGES)]

    @pl.when(step == 0)
    def _():
        for cp in copies(bi, ci, slot):
            cp.start()

    @pl.when(step + 1 < pl.num_programs(0) * nch)
    def _():
        nxt = step + 1
        for cp in copies(nxt // nch, nxt % nch, 1 - slot):
            cp.start()

    for cp in copies(bi, ci, slot):
        cp.wait()

    for j in range(SEL_PAGES // PAGES_PER_BLOCK):
        blk = buf_ref[slot, j * PAGES_PER_BLOCK:(j + 1) * PAGES_PER_BLOCK]
        ksum_ref[ci * (SEL_PAGES // PAGES_PER_BLOCK) + j] = jnp.sum(
            blk.reshape(MOBA_BLOCK, ATT_HEADS, ATT_DIM), axis=0)

    @pl.when(ci == nch - 1)
    def _():
        kmean = ksum_ref[...] * (1.0 / MOBA_BLOCK)
        gate = jnp.sum(kmean * q_ref[...], axis=-1, keepdims=True)
        idx = lax.broadcasted_iota(I32, (nblk, ATT_HEADS, 1), 0)
        rank = _topk_rank(gate, 0, nblk, idx)
        idx_f = idx.astype(F32)
        for j in range(MOBA_TOPK):
            pick = jnp.sum(jnp.where(rank == float(j), idx_f, 0.0), axis=0)
            sel_ref[0, j] = pick.astype(I32)


def _moba_select(page_table, q3, cache_k):
    b, n_pages = page_table.shape
    nch = n_pages // SEL_PAGES
    nblk = n_pages // PAGES_PER_BLOCK
    grid_spec = pltpu.PrefetchScalarGridSpec(
        num_scalar_prefetch=1,
        grid=(b, nch),
        in_specs=[pl.BlockSpec((1, ATT_HEADS, ATT_DIM), lambda i, j, pt: (i, 0, 0)),
                  pl.BlockSpec(memory_space=pl.ANY)],
        out_specs=pl.BlockSpec((1, MOBA_TOPK, ATT_HEADS, 1), lambda i, j, pt: (i, 0, 0, 0)),
        scratch_shapes=[pltpu.VMEM((2, SEL_PAGES, PAGE_SIZE, ATT_HEADS, ATT_DIM), F32),
                        pltpu.VMEM((nblk, ATT_HEADS, ATT_DIM), F32),
                        pltpu.SemaphoreType.DMA((2,))],
    )
    return pl.pallas_call(
        functools.partial(_moba_select_kernel, n_pages=n_pages),
        grid_spec=grid_spec,
        out_shape=jax.ShapeDtypeStruct((b, MOBA_TOPK, ATT_HEADS, 1), I32),
        compiler_params=pltpu.CompilerParams(dimension_semantics=("arbitrary", "arbitrary"),
                                             vmem_limit_bytes=VMEM_LIMIT),
        name="moba_select",
    )(page_table, q3, cache_k).reshape(b, MOBA_TOPK, ATT_HEADS)


ATT_PAGES = MOBA_TOPK * PAGES_PER_BLOCK


def _moba_attend_kernel(pt_ref, sel_ref, q_ref, kn_ref, vn_ref, kc_ref, vc_ref, o_ref, kbuf, vbuf, sem_ref):
    bi = pl.program_id(0)
    hi = pl.program_id(1)
    nh = pl.num_programs(1)
    step = bi * nh + hi
    slot = step % 2
    scale = ATT_DIM ** -0.5

    def copies(b_idx, h_idx, slot_idx):
        out = []
        for j in range(MOBA_TOPK):
            blk = sel_ref[b_idx, j, h_idx]
            for p in range(PAGES_PER_BLOCK):
                page = pt_ref[b_idx, blk * PAGES_PER_BLOCK + p]
                dst = j * PAGES_PER_BLOCK + p
                out.append(pltpu.make_async_copy(kc_ref.at[0, page], kbuf.at[slot_idx, dst], sem_ref.at[0, slot_idx]))
                out.append(pltpu.make_async_copy(vc_ref.at[0, page], vbuf.at[slot_idx, dst], sem_ref.at[1, slot_idx]))
        return out

    @pl.when(step == 0)
    def _():
        for cp in copies(bi, hi, slot):
            cp.start()

    @pl.when(step + 1 < pl.num_programs(0) * nh)
    def _():
        nxt = step + 1
        for cp in copies(nxt // nh, nxt % nh, 1 - slot):
            cp.start()

    for cp in copies(bi, hi, slot):
        cp.wait()

    head_rows = lax.broadcasted_iota(I32, (1, ATT_HEADS, 1), 1)
    head_mask = head_rows == hi
    q = q_ref[...]
    n_rows = ATT_PAGES * PAGE_SIZE
    kk = kbuf[slot].reshape(n_rows, ATT_HEADS, ATT_DIM)
    vv = vbuf[slot].reshape(n_rows, ATT_HEADS, ATT_DIM)
    s = jnp.sum(kk * q, axis=-1, keepdims=True) * scale
    s_own = jnp.sum(kn_ref[...] * q, axis=-1, keepdims=True) * scale
    mx = jnp.maximum(jnp.max(s, axis=0, keepdims=True), s_own)
    p = jnp.exp(s - mx)
    p_own = jnp.exp(s_own - mx)
    den = jnp.sum(p, axis=0, keepdims=True) + p_own
    acc = (jnp.sum(p * vv, axis=0, keepdims=True) + p_own * vn_ref[...]) / den

    @pl.when(hi == 0)
    def _():
        o_ref[...] = jnp.zeros_like(o_ref)

    o_ref[...] = jnp.where(head_mask, acc, o_ref[...])


def _moba_attend(page_table, sel, q3, kn3, vn3, cache_k, cache_v):
    b = page_table.shape[0]
    row = pl.BlockSpec((1, ATT_HEADS, ATT_DIM), lambda i, j, pt, sl: (i, 0, 0))
    grid_spec = pltpu.PrefetchScalarGridSpec(
        num_scalar_prefetch=2,
        grid=(b, ATT_HEADS),
        in_specs=[row, row, row, pl.BlockSpec(memory_space=pl.ANY), pl.BlockSpec(memory_space=pl.ANY)],
        out_specs=row,
        scratch_shapes=[pltpu.VMEM((2, ATT_PAGES, PAGE_SIZE, ATT_HEADS, ATT_DIM), F32),
                        pltpu.VMEM((2, ATT_PAGES, PAGE_SIZE, ATT_HEADS, ATT_DIM), F32),
                        pltpu.SemaphoreType.DMA((2, 2))],
    )
    return pl.pallas_call(
        _moba_attend_kernel,
        grid_spec=grid_spec,
        out_shape=jax.ShapeDtypeStruct((b, ATT_HEADS, ATT_DIM), F32),
        compiler_params=pltpu.CompilerParams(dimension_semantics=("arbitrary", "arbitrary"),
                                             vmem_limit_bytes=VMEM_LIMIT),
        name="moba_attend",
    )(page_table, sel, q3, kn3, vn3, cache_k, cache_v)


FFN_COLS = 256


def _post_kernel(x_ref, ohg_ref, oatt_ref, prev_ref, wo_ref, gpm_ref, gpf_ref, gqf_ref, wi_ref, cw_ref, cb_ref,
                 wf_ref, y_ref, conv_ref, gate_ref, *, decode):
    li = pl.program_id(1)
    tl = x_ref.shape[1]
    d_ff = wf_ref.shape[0]
    mix = _dot(ohg_ref[0].astype(BF16), wo_ref[:HG_WIDTH, :]) + _dot(oatt_ref[0].astype(BF16), wo_ref[HG_WIDTH:, :])
    x1 = x_ref[0] + _rms(mix, gpm_ref[...])
    h2 = _rms(x1, gpf_ref[...]).astype(BF16)

    if not decode:
        @pl.when(li == 0)
        def _():
            gate_ref[0:SUBLANES, :] = jnp.zeros((SUBLANES, d_ff), F32)
            gate_ref[SUBLANES - 2:SUBLANES - 1, :] = prev_ref[0, 0]
            gate_ref[SUBLANES - 1:SUBLANES, :] = prev_ref[0, 1]

    acc = jnp.zeros((tl, x_ref.shape[2]), F32)
    for c0 in range(0, d_ff, FFN_COLS):
        cols = slice(c0, c0 + FFN_COLS)
        gate = _dot(h2, wi_ref[:, cols])
        up = _dot(h2, wi_ref[:, d_ff + c0:d_ff + c0 + FFN_COLS])
        w0 = cw_ref[0:1, cols]
        w1 = cw_ref[1:2, cols]
        w2 = cw_ref[2:3, cols]
        if decode:
            conv = prev_ref[0, 0, :, cols] * w0 + prev_ref[0, 1, :, cols] * w1 + gate * w2 + cb_ref[:, cols]
            conv_ref[0, 0, :, cols] = prev_ref[0, 1, :, cols]
            conv_ref[0, 1, :, cols] = gate
        else:
            gate_ref[SUBLANES:SUBLANES + tl, cols] = gate
            conv = (gate_ref[SUBLANES - 2:SUBLANES - 2 + tl, cols] * w0
                    + gate_ref[SUBLANES - 1:SUBLANES - 1 + tl, cols] * w1 + gate * w2 + cb_ref[:, cols])
        act = jax.nn.gelu(conv, approximate=True) * up
        acc = acc + _dot(act.astype(BF16), wf_ref[cols, :])
    y_ref[0] = x1 + _rms(acc, gqf_ref[...])

    if not decode:
        last = gate_ref[tl:tl + SUBLANES, :]
        gate_ref[0:SUBLANES, :] = last

        @pl.when(li == pl.num_programs(1) - 1)
        def _():
            conv_ref[0, 0] = last[SUBLANES - 2:SUBLANES - 1, :]
            conv_ref[0, 1] = last[SUBLANES - 1:SUBLANES, :]


def _post(x3, ohg3, oatt3, prev, wo, gpm, gpf, gqf, wi, cw, cb, wf, tl, decode):
    b, l, d = x3.shape
    d_ff = wf.shape[0]
    row = lambda width: pl.BlockSpec((1, tl, width), lambda i, j: (i, j, 0))
    const = lambda shape: pl.BlockSpec(shape, lambda i, j: (0,) * len(shape), pipeline_mode=pl.Buffered(1))
    if decode:
        state = pl.BlockSpec((1, CONV_WIDTH - 1, tl, d_ff), lambda i, j: (i, 0, j, 0))
    else:
        state = pl.BlockSpec((1, CONV_WIDTH - 1, 1, d_ff), lambda i, j: (i, 0, 0, 0))
    return pl.pallas_call(
        functools.partial(_post_kernel, decode=decode),
        grid=(b, l // tl),
        in_specs=[row(d), row(HG_WIDTH), row(ATT_WIDTH), state,
                  const(wo.shape), const((1, d)), const((1, d)), const((1, d)),
                  const(wi.shape), const(cw.shape), const((1, d_ff)), const(wf.shape)],
        out_specs=[row(d), state],
        out_shape=[jax.ShapeDtypeStruct((b, l, d), F32),
                   jax.ShapeDtypeStruct(prev.shape, F32)],
        scratch_shapes=[pltpu.VMEM((tl + SUBLANES, d_ff), F32)],
        compiler_params=pltpu.CompilerParams(dimension_semantics=("arbitrary", "arbitrary"),
                                             vmem_limit_bytes=VMEM_LIMIT),
        name="post_decode" if decode else "post_prompt",
    )(x3, ohg3, oatt3, prev, wo, gpm, gpf, gqf, wi, cw, cb, wf)


INPROJ_ROWS = 512
POST_ROWS = 512
HGRN_CHUNK = 128


def kernel(x_prompt, x_sample, cache_k, cache_v, state_hgrn, state_conv, page_table, w_in, w_out, hg_lb_logits,
           hg_norm, g_pre_mix, g_post_mix, g_pre_ffn, g_post_ffn, w_ffn_in, conv_w, conv_b, w_ffn_out):
    depth = w_in.shape[0]
    assert depth == 1, "single-layer step"
    bp, seq, d = x_prompt.shape
    bs, t_dec, _ = x_sample.shape
    assert t_dec == 1
    n_pages = page_table.shape[1]
    past = n_pages * PAGE_SIZE
    lb_all = jnp.cumsum(jax.nn.softmax(hg_lb_logits.astype(F32), axis=0), axis=0)

    layer = 0
    lb = lb_all[layer][None, :]
    ng = hg_norm[layer][None, :]
    w_in_b = w_in[layer].astype(BF16)
    w_out_b = w_out[layer].astype(BF16)
    w_fi_b = w_ffn_in[layer].astype(BF16)
    w_fo_b = w_ffn_out[layer].astype(BF16)
    gains = [g[layer][None, :] for g in (g_pre_mix, g_post_mix, g_pre_ffn, g_post_ffn)]
    cw = conv_w[layer]
    cb = conv_b[layer][None, :]
    post_w = (w_out_b, gains[1], gains[2], gains[3], w_fi_b, cw, cb, w_fo_b)

    tm = min(INPROJ_ROWS, seq)
    tabs_p = _rope_tables(jnp.arange(seq, dtype=I32))
    zh, q, k, v = _inproj(x_prompt.reshape(bp * seq, d), gains[0], w_in_b, tabs_p, tm)
    o_hg, s_p = _hgrn_prompt(zh.reshape(bp, seq, 4 * HG_WIDTH), lb, ng, min(HGRN_CHUNK, seq))
    q3, k3, v3 = (a.reshape(bp, seq, ATT_WIDTH) for a in (q, k, v))
    o_att = _moba_prompt(q3, k3, v3)
    d_ff = w_ffn_out.shape[1]
    zero_prev = jnp.zeros((bp, CONV_WIDTH - 1, 1, d_ff), F32)
    y_p, conv_p = _post(x_prompt, o_hg, o_att, zero_prev, *post_w, tl=min(POST_ROWS, seq), decode=False)
    conv_p = conv_p.reshape(bp, CONV_WIDTH - 1, d_ff)

    tabs_s = _rope_tables(jnp.full((bs,), past, dtype=I32))
    zh_s, q_s, k_s, v_s = _inproj(x_sample.reshape(bs, d), gains[0], w_in_b, tabs_s, bs)
    o_hg_s, s_s = _hgrn_step(zh_s.reshape(bs, 1, 4 * HG_WIDTH), lb, ng, state_hgrn[layer])
    heads = lambda a: a.reshape(bs, ATT_HEADS, ATT_DIM)
    sel = _moba_select(page_table, heads(q_s), cache_k)
    o_att_s = _moba_attend(page_table, sel, heads(q_s), heads(k_s), heads(v_s), cache_k, cache_v)
    rows3 = lambda a: a.reshape(1, bs, a.shape[-1])
    prev_s = jnp.transpose(state_conv[layer], (1, 0, 2))[None]
    y_s, conv_s = _post(rows3(x_sample), rows3(o_hg_s), o_att_s.reshape(1, bs, ATT_WIDTH), prev_s, *post_w,
                        tl=bs, decode=True)
    y_s = y_s.reshape(bs, 1, d)
    conv_s = jnp.transpose(conv_s[0], (1, 0, 2))

    kv_p = lambda a: a.reshape(1, bp, seq, ATT_HEADS, ATT_DIM)
    kv_s = lambda a: a.reshape(1, bs, 1, ATT_HEADS, ATT_DIM)
    return (y_p, y_s, kv_p(k), kv_p(v), s_p[None], conv_p[None],
            kv_s(k_s), kv_s(v_s), s_s[None], conv_s[None])
```

```python
import functools

import jax
import jax.numpy as jnp
from jax import lax
from jax.experimental import pallas as pl
from jax.experimental.pallas import tpu as pltpu

F32 = jnp.float32
BF16 = jnp.bfloat16
I32 = jnp.int32

HG_HEADS = 4
HG_DIM = 128
HG_WIDTH = HG_HEADS * HG_DIM
ATT_HEADS = 8
ATT_DIM = 64
ATT_WIDTH = ATT_HEADS * ATT_DIM
MOBA_BLOCK = 256
MOBA_TOPK = 3
ROPE_THETA = 500000.0
ROPE_DIMS = ATT_DIM // 4
ROPE_HALF = ROPE_DIMS // 2
CONV_WIDTH = 3
NORM_EPS = 1e-6
PAGE_SIZE = 128
PAGES_PER_BLOCK = MOBA_BLOCK // PAGE_SIZE

LANES = 128
SUBLANES = 8
VMEM_LIMIT = 56 * 1024 * 1024

NEG_INF = float("-inf")
_NT = (((1,), (1,)), ((), ()))
_TN = (((0,), (0,)), ((), ()))


def _dot(a, b):
    return jnp.dot(a, b, preferred_element_type=F32)


def _rms(x, g):
    return x * lax.rsqrt(jnp.mean(x * x, axis=-1, keepdims=True) + NORM_EPS) * g


def _sigmoid(x):
    return 1.0 / (1.0 + jnp.exp(-x))


def _split3(x):
    x1 = x.astype(BF16)
    r1 = x - x1.astype(F32)
    x2 = r1.astype(BF16)
    r2 = r1 - x2.astype(F32)
    return x1, x2, r2.astype(BF16)


def _dot_f32(a, b, dims):
    a1, a2, a3 = _split3(a)
    b1, b2, b3 = _split3(b)
    dg = functools.partial(lax.dot_general, dimension_numbers=dims, preferred_element_type=F32)
    small = dg(a1, b3) + dg(a3, b1) + dg(a2, b2)
    mid = dg(a1, b2) + dg(a2, b1)
    return dg(a1, b1) + (mid + small)


def _inproj_kernel(x_ref, g_ref, w_ref, cos_ref, sa_ref, sb_ref, zh_ref, q_ref, k_ref, v_ref):
    h = _rms(x_ref[...], g_ref[...]).astype(BF16)
    hgw = 4 * HG_WIDTH
    for c in range(4):
        zh_ref[:, c * HG_WIDTH:(c + 1) * HG_WIDTH] = _dot(h, w_ref[:, c * HG_WIDTH:(c + 1) * HG_WIDTH])
    cos = cos_ref[...]
    sa = sa_ref[...]
    sb = sb_ref[...]
    for out_ref, off in ((q_ref, hgw), (k_ref, hgw + ATT_WIDTH)):
        a = _dot(h, w_ref[:, off:off + ATT_WIDTH])
        for j in range(ATT_WIDTH // LANES):
            aj = a[:, j * LANES:(j + 1) * LANES]
            out_ref[:, j * LANES:(j + 1) * LANES] = (
                aj * cos + pltpu.roll(aj, LANES - ROPE_HALF, 1) * sa + pltpu.roll(aj, ROPE_HALF, 1) * sb)
    v_ref[...] = _dot(h, w_ref[:, hgw + 2 * ATT_WIDTH:hgw + 3 * ATT_WIDTH])


def _rope_tables(pos):
    inv = jnp.power(jnp.float32(ROPE_THETA), -jnp.arange(ROPE_HALF, dtype=F32) * (2.0 / ROPE_DIMS))
    ang = pos.astype(F32)[:, None] * inv[None, :]
    cos = jnp.cos(ang)
    sin = jnp.sin(ang)
    n = pos.shape[0]
    rest = ATT_DIM - ROPE_DIMS
    cos_h = jnp.concatenate([cos, cos, jnp.ones((n, rest), F32)], axis=1)
    sa_h = jnp.concatenate([-sin, jnp.zeros((n, ATT_DIM - ROPE_HALF), F32)], axis=1)
    sb_h = jnp.concatenate([jnp.zeros((n, ROPE_HALF), F32), sin, jnp.zeros((n, rest), F32)], axis=1)
    rep = LANES // ATT_DIM
    return tuple(jnp.tile(t, (1, rep)) for t in (cos_h, sa_h, sb_h))


def _inproj(x2d, g, w_bf16, tables, tm):
    rows, d = x2d.shape
    ncols = w_bf16.shape[1]
    npos = tables[0].shape[0] // tm
    tab_spec = pl.BlockSpec((tm, LANES), lambda i: (i % npos, 0))
    row_spec = lambda width: pl.BlockSpec((tm, width), lambda i: (i, 0))
    return pl.pallas_call(
        _inproj_kernel,
        grid=(rows // tm,),
        in_specs=[row_spec(d),
                  pl.BlockSpec((1, d), lambda i: (0, 0)),
                  pl.BlockSpec((d, ncols), lambda i: (0, 0), pipeline_mode=pl.Buffered(1)),
                  tab_spec, tab_spec, tab_spec],
        out_specs=[row_spec(4 * HG_WIDTH), row_spec(ATT_WIDTH), row_spec(ATT_WIDTH), row_spec(ATT_WIDTH)],
        out_shape=[jax.ShapeDtypeStruct((rows, 4 * HG_WIDTH), F32),
                   jax.ShapeDtypeStruct((rows, ATT_WIDTH), F32),
                   jax.ShapeDtypeStruct((rows, ATT_WIDTH), F32),
                   jax.ShapeDtypeStruct((rows, ATT_WIDTH), F32)],
        compiler_params=pltpu.CompilerParams(dimension_semantics=("arbitrary",), vmem_limit_bytes=VMEM_LIMIT),
        name="inproj",
    )(x2d, g, w_bf16, *tables)


def _hgrn_gates(zh_row, lb, h):
    sl = lambda part: slice(part * HG_WIDTH + h * HG_DIM, part * HG_WIDTH + (h + 1) * HG_DIM)
    hq = zh_row[:, sl(0)]
    hf = zh_row[:, sl(1)]
    v = zh_row[:, sl(2)]
    hg = zh_row[:, sl(3)]
    q = hq * _sigmoid(hq)
    f = lb + (1.0 - lb) * _sigmoid(hf)
    return q, f, 1.0 - f, v, hg * _sigmoid(hg)


def _level_reference(b, b3, sub, level, c):
    m = 1 << level
    if m >= SUBLANES:
        pieces = []
        for grp in range(c // (2 * m)):
            r = grp * 2 * m + m - 1
            pieces.append(jnp.broadcast_to(b[r:r + 1, :], (2 * m, HG_DIM)))
        return pieces[0] if len(pieces) == 1 else jnp.concatenate(pieces, axis=0)
    beta3 = None
    for g0 in range(0, SUBLANES, 2 * m):
        r = g0 + m - 1
        piece = jnp.broadcast_to(b3[:, r:r + 1, :], b3.shape)
        beta3 = piece if beta3 is None else jnp.where(sub >= g0, piece, beta3)
    return beta3.reshape(c, HG_DIM)


def _hgrn_prompt_kernel(zh_ref, lb_ref, ng_ref, o_ref, s_out_ref, st_ref, lv_ref, *, chunk):
    c = chunk
    ci = pl.program_id(1)
    nlev = c.bit_length() - 1
    rows = lax.broadcasted_iota(I32, (c, c), 0)
    cols = lax.broadcasted_iota(I32, (c, c), 1)

    @pl.when(ci == 0)
    def _():
        st_ref[...] = jnp.zeros_like(st_ref)
        x = rows ^ cols
        lev = jnp.zeros((c, c), I32)
        for j in range(1, nlev):
            lev = lev + jnp.where(x >= (1 << j), 1, 0)
        lv_ref[...] = jnp.where(cols < rows, lev, jnp.where(cols == rows, -1, -2))

    lv = lv_ref[...]
    tri = jnp.where(cols <= rows, 1.0, 0.0).astype(BF16)
    sub = lax.broadcasted_iota(I32, (c // SUBLANES, SUBLANES, HG_DIM), 1)
    zh = zh_ref[0]
    ng = ng_ref[...]
    for h in range(HG_HEADS):
        lb = lb_ref[:, h * HG_DIM:(h + 1) * HG_DIM]
        q, f, k, v, og = _hgrn_gates(zh, lb, h)
        g1, g2, g3 = _split3(jnp.log(f))
        b = _dot(tri, g1) + (_dot(tri, g2) + _dot(tri, g3))
        b3 = b.reshape(c // SUBLANES, SUBLANES, HG_DIM)
        vb = v.astype(BF16)

        a = jnp.where(lv == -1, jnp.sum(q * k, axis=-1, keepdims=True), 0.0)
        for level in range(nlev):
            w = jnp.exp(-jnp.abs(b - _level_reference(b, b3, sub, level, c)))
            a_l = lax.dot_general((q * w).astype(BF16), (k * w).astype(BF16), _NT, preferred_element_type=F32)
            a = jnp.where(lv == level, a_l, a)

        st = st_ref[h]
        o = _dot(a.astype(BF16), vb) + lax.dot_general(
            (q * jnp.exp(b)).astype(BF16), st.astype(BF16), _NT, preferred_element_type=F32)
        b_end = b[c - 1:c, :]
        k_end = (k * jnp.exp(b_end - b)).astype(BF16)
        st_new = st * jnp.exp(b_end) + lax.dot_general(vb, k_end, _TN, preferred_element_type=F32)
        st_ref[h] = st_new
        o_ref[0, :, h * HG_DIM:(h + 1) * HG_DIM] = _rms(o, ng) * og

    @pl.when(ci == pl.num_programs(1) - 1)
    def _():
        for h in range(HG_HEADS):
            s_out_ref[0, h] = st_ref[h].T


def _hgrn_prompt(zh3, lb, ng, chunk):
    b, l, _ = zh3.shape
    return pl.pallas_call(
        functools.partial(_hgrn_prompt_kernel, chunk=chunk),
        grid=(b, l // chunk),
        in_specs=[pl.BlockSpec((1, chunk, 4 * HG_WIDTH), lambda i, j: (i, j, 0)),
                  pl.BlockSpec((1, HG_WIDTH), lambda i, j: (0, 0)),
                  pl.BlockSpec((1, HG_DIM), lambda i, j: (0, 0))],
        out_specs=[pl.BlockSpec((1, chunk, HG_WIDTH), lambda i, j: (i, j, 0)),
                   pl.BlockSpec((1, HG_HEADS, HG_DIM, HG_DIM), lambda i, j: (i, 0, 0, 0))],
        out_shape=[jax.ShapeDtypeStruct((b, l, HG_WIDTH), F32),
                   jax.ShapeDtypeStruct((b, HG_HEADS, HG_DIM, HG_DIM), F32)],
        scratch_shapes=[pltpu.VMEM((HG_HEADS, HG_DIM, HG_DIM), F32), pltpu.VMEM((chunk, chunk), I32)],
        compiler_params=pltpu.CompilerParams(dimension_semantics=("arbitrary", "arbitrary"),
                                             vmem_limit_bytes=VMEM_LIMIT),
        name="hgrn_prompt",
    )(zh3, lb, ng)


def _hgrn_step_kernel(zh_ref, lb_ref, ng_ref, s_ref, o_ref, s_out_ref):
    zh = zh_ref[0]
    ng = ng_ref[...]
    eye = (lax.broadcasted_iota(I32, (HG_DIM, HG_DIM), 0) == lax.broadcasted_iota(I32, (HG_DIM, HG_DIM), 1))

    def column(row):
        return jnp.sum(jnp.where(eye, row, 0.0), axis=1, keepdims=True)

    for h in range(HG_HEADS):
        lb = lb_ref[:, h * HG_DIM:(h + 1) * HG_DIM]
        q, f, k, v, og = _hgrn_gates(zh, lb, h)
        s_new = column(f) * s_ref[0, h] + column(k) * v
        s_out_ref[0, h] = s_new
        o = jnp.sum(column(q) * s_new, axis=0, keepdims=True)
        o_ref[0, :, h * HG_DIM:(h + 1) * HG_DIM] = _rms(o, ng) * og


def _hgrn_step(zh3, lb, ng, state):
    b = zh3.shape[0]
    return pl.pallas_call(
        _hgrn_step_kernel,
        grid=(b,),
        in_specs=[pl.BlockSpec((1, 1, 4 * HG_WIDTH), lambda i: (i, 0, 0)),
                  pl.BlockSpec((1, HG_WIDTH), lambda i: (0, 0)),
                  pl.BlockSpec((1, HG_DIM), lambda i: (0, 0)),
                  pl.BlockSpec((1, HG_HEADS, HG_DIM, HG_DIM), lambda i: (i, 0, 0, 0))],
        out_specs=[pl.BlockSpec((1, 1, HG_WIDTH), lambda i: (i, 0, 0)),
                   pl.BlockSpec((1, HG_HEADS, HG_DIM, HG_DIM), lambda i: (i, 0, 0, 0))],
        out_shape=[jax.ShapeDtypeStruct((b, 1, HG_WIDTH), F32),
                   jax.ShapeDtypeStruct(state.shape, F32)],
        compiler_params=pltpu.CompilerParams(dimension_semantics=("arbitrary",)),
        name="hgrn_step",
    )(zh3, lb, ng, state)


def _topk_rank(gm, axis, n_candidates, idx):
    rank = jnp.zeros(gm.shape, F32)
    for m in range(n_candidates):
        other = lax.slice_in_dim(gm, m, m + 1, axis=axis)
        tie = jnp.where(other == gm, jnp.where(idx > m, 1.0, 0.0), 0.0)
        rank = rank + jnp.where(other > gm, 1.0, tie)
    return rank


def _moba_prompt_kernel(q_ref, k_ref, v_ref, o_ref):
    l = q_ref.shape[1]
    nb = l // MOBA_BLOCK
    scale = ATT_DIM ** -0.5
    blk_idx = lax.broadcasted_iota(I32, (l, LANES), 1)
    own_blk = lax.broadcasted_iota(I32, (l, LANES), 0) // MOBA_BLOCK
    causal = (lax.broadcasted_iota(I32, (MOBA_BLOCK, MOBA_BLOCK), 1)
              <= lax.broadcasted_iota(I32, (MOBA_BLOCK, MOBA_BLOCK), 0))
    for hh in range(LANES // ATT_DIM):
        lanes = slice(hh * ATT_DIM, (hh + 1) * ATT_DIM)
        qh = q_ref[0, :, lanes]
        kh = k_ref[0, :, lanes]
        vb = v_ref[0, :, lanes].astype(BF16)
        kmean = jnp.sum(kh.reshape(nb, MOBA_BLOCK, ATT_DIM), axis=1) * (1.0 / MOBA_BLOCK)
        kmean = jnp.concatenate([kmean, jnp.zeros((LANES - nb, ATT_DIM), F32)], axis=0)
        gate = _dot_f32(qh, kmean, _NT)
        valid = blk_idx < own_blk
        rank = _topk_rank(jnp.where(valid, gate, NEG_INF), 1, nb, blk_idx)
        sel = jnp.where(valid, jnp.where(rank < MOBA_TOPK, 1.0, 0.0), 0.0)
        qb = (qh * scale).astype(BF16)
        kb = kh.astype(BF16)
        for i in range(nb):
            rows = slice(i * MOBA_BLOCK, (i + 1) * MOBA_BLOCK)
            n_keys = (i + 1) * MOBA_BLOCK
            s = lax.dot_general(qb[rows], kb[:n_keys], _NT, preferred_element_type=F32)
            pieces = [jnp.where(sel[rows, j:j + 1] > 0.5, s[:, j * MOBA_BLOCK:(j + 1) * MOBA_BLOCK], NEG_INF)
                      for j in range(i)]
            pieces.append(jnp.where(causal, s[:, i * MOBA_BLOCK:], NEG_INF))
            sm = pieces[0] if i == 0 else jnp.concatenate(pieces, axis=1)
            p = jnp.exp(sm - jnp.max(sm, axis=1, keepdims=True))
            den = jnp.sum(p, axis=1, keepdims=True)
            o_ref[0, rows, lanes] = _dot(p.astype(BF16), vb[:n_keys]) / den


def _moba_prompt(q3, k3, v3):
    b, l, _ = q3.shape
    spec = pl.BlockSpec((1, l, LANES), lambda i, j: (i, 0, j))
    return pl.pallas_call(
        _moba_prompt_kernel,
        grid=(b, ATT_WIDTH // LANES),
        in_specs=[spec, spec, spec],
        out_specs=spec,
        out_shape=jax.ShapeDtypeStruct((b, l, ATT_WIDTH), F32),
        compiler_params=pltpu.CompilerParams(dimension_semantics=("arbitrary", "arbitrary"),
                                             vmem_limit_bytes=VMEM_LIMIT),
        name="moba_prompt",
    )(q3, k3, v3)


SEL_PAGES = 8


def _column(row, eye):
    return jnp.sum(jnp.where(eye, row, 0.0), axis=1, keepdims=True)


def _moba_scores_kernel(pt_ref, q_ref, kc_ref, s_ref, sel_ref, buf_ref, sem_ref, *, n_pages):
    bi = pl.program_id(0)
    ci = pl.program_id(1)
    nch = pl.num_programs(1)
    step = bi * nch + ci
    slot = step % 2
    nblk = n_pages // PAGES_PER_BLOCK

    def copies(b_idx, c_idx, slot_idx):
        return [pltpu.make_async_copy(kc_ref.at[0, pt_ref[b_idx, c_idx * SEL_PAGES + p]],
                                      buf_ref.at[slot_idx, p], sem_ref.at[slot_idx])
                for p in range(SEL_PAGES)]

    @pl.when(step == 0)
    def _():
        for cp in copies(bi, ci, slot):
            cp.start()

    @pl.when(step + 1 < pl.num_programs(0) * nch)
    def _():
        nxt = step + 1
        for cp in copies(nxt // nch, nxt % nch, 1 - slot):
            cp.start()

    for cp in copies(bi, ci, slot):
        cp.wait()

    eye = (lax.broadcasted_iota(I32, (ATT_DIM, ATT_DIM), 0) == lax.broadcasted_iota(I32, (ATT_DIM, ATT_DIM), 1))
    q = q_ref[0]
    q_cols = [_column(q[h:h + 1, :], eye) for h in range(ATT_HEADS)]
    for p in range(SEL_PAGES):
        rows = [jnp.sum(buf_ref[slot, p, h] * q_cols[h], axis=0, keepdims=True) for h in range(ATT_HEADS)]
        blk = (ci * SEL_PAGES + p) // PAGES_PER_BLOCK
        half = p % PAGES_PER_BLOCK
        s_ref[0, pl.ds(pl.multiple_of(blk * ATT_HEADS, ATT_HEADS), ATT_HEADS),
              half * PAGE_SIZE:(half + 1) * PAGE_SIZE] = jnp.concatenate(rows, axis=0)

    @pl.when(ci == nch - 1)
    def _():
        lane = lax.broadcasted_iota(I32, (ATT_HEADS, LANES), 1)
        gate = jnp.full((ATT_HEADS, LANES), NEG_INF, F32)
        for n in range(nblk):
            g_n = jnp.sum(s_ref[0, n * ATT_HEADS:(n + 1) * ATT_HEADS, :], axis=1, keepdims=True) * (1.0 / MOBA_BLOCK)
            gate = jnp.where(lane == n, g_n, gate)
        rank = _topk_rank(gate, 1, nblk, lane)
        lane_f = lane.astype(F32)
        for j in range(MOBA_TOPK):
            pick = jnp.sum(jnp.where(rank == float(j), jnp.where(lane < nblk, lane_f, 0.0), 0.0),
                           axis=1, keepdims=True)
            sel_ref[0, j] = pick.astype(I32)


def _moba_scores(page_table, q3, cache_kt):
    b, n_pages = page_table.shape
    nch = n_pages // SEL_PAGES
    nblk = n_pages // PAGES_PER_BLOCK
    grid_spec = pltpu.PrefetchScalarGridSpec(
        num_scalar_prefetch=1,
        grid=(b, nch),
        in_specs=[pl.BlockSpec((1, ATT_HEADS, ATT_DIM), lambda i, j, pt: (i, 0, 0)),
                  pl.BlockSpec(memory_space=pl.ANY)],
        out_specs=[pl.BlockSpec((1, nblk * ATT_HEADS, MOBA_BLOCK), lambda i, j, pt: (i, 0, 0)),
                   pl.BlockSpec((1, MOBA_TOPK, ATT_HEADS, 1), lambda i, j, pt: (i, 0, 0, 0))],
        scratch_shapes=[pltpu.VMEM((2, SEL_PAGES, ATT_HEADS, ATT_DIM, PAGE_SIZE), F32),
                        pltpu.SemaphoreType.DMA((2,))],
    )
    scores, sel = pl.pallas_call(
        functools.partial(_moba_scores_kernel, n_pages=n_pages),
        grid_spec=grid_spec,
        out_shape=[jax.ShapeDtypeStruct((b, nblk * ATT_HEADS, MOBA_BLOCK), F32),
                   jax.ShapeDtypeStruct((b, MOBA_TOPK, ATT_HEADS, 1), I32)],
        compiler_params=pltpu.CompilerParams(dimension_semantics=("arbitrary", "arbitrary"),
                                             vmem_limit_bytes=VMEM_LIMIT),
        name="moba_scores",
    )(page_table, q3, cache_kt)
    return scores, sel.reshape(b, MOBA_TOPK, ATT_HEADS)


ATT_PAGES = MOBA_TOPK * PAGES_PER_BLOCK


def _moba_attend_kernel(pt_ref, sel_ref, s_ref, q_ref, kn_ref, vn_ref, vc_ref, o_ref, vbuf, sem_ref):
    bi = pl.program_id(0)
    slot = bi % 2
    scale = ATT_DIM ** -0.5

    def copies(b_idx, slot_idx):
        out = []
        for h in range(ATT_HEADS):
            for j in range(MOBA_TOPK):
                blk = sel_ref[b_idx, j, h]
                for p in range(PAGES_PER_BLOCK):
                    page = pt_ref[b_idx, blk * PAGES_PER_BLOCK + p]
                    out.append(pltpu.make_async_copy(vc_ref.at[0, page, h],
                                                     vbuf.at[slot_idx, h, j * PAGES_PER_BLOCK + p],
                                                     sem_ref.at[slot_idx]))
        return out

    @pl.when(bi == 0)
    def _():
        for cp in copies(bi, slot):
            cp.start()

    @pl.when(bi + 1 < pl.num_programs(0))
    def _():
        for cp in copies(bi + 1, 1 - slot):
            cp.start()

    for cp in copies(bi, slot):
        cp.wait()

    eye = (lax.broadcasted_iota(I32, (ATT_DIM, LANES), 0) == lax.broadcasted_iota(I32, (ATT_DIM, LANES), 1))
    s_own_all = jnp.sum(q_ref[0] * kn_ref[0], axis=1, keepdims=True) * scale
    for h in range(ATT_HEADS):
        s_blk = [s_ref[0, pl.ds(sel_ref[bi, j, h] * ATT_HEADS + h, 1), :] * scale for j in range(MOBA_TOPK)]
        s_own = s_own_all[h:h + 1, :]
        mx = s_own
        for s_j in s_blk:
            mx = jnp.maximum(mx, jnp.max(s_j, axis=1, keepdims=True))
        p_blk = [jnp.exp(s_j - mx) for s_j in s_blk]
        p_own = jnp.exp(s_own - mx)
        den = p_own
        acc = jnp.zeros((ATT_DIM, PAGE_SIZE), F32)
        for j, p_j in enumerate(p_blk):
            den = den + jnp.sum(p_j, axis=1, keepdims=True)
            for p in range(PAGES_PER_BLOCK):
                acc = acc + p_j[:, p * PAGE_SIZE:(p + 1) * PAGE_SIZE] * vbuf[slot, h, j * PAGES_PER_BLOCK + p]
        o_col = jnp.sum(acc, axis=1, keepdims=True)
        o_row = jnp.sum(jnp.where(eye, o_col, 0.0), axis=0, keepdims=True)[:, :ATT_DIM]
        o_ref[0, h:h + 1, :] = (o_row + p_own * vn_ref[0, h:h + 1, :]) / den


def _moba_attend(page_table, sel, scores, q3, kn3, vn3, cache_vt):
    b = page_table.shape[0]
    row = pl.BlockSpec((1, ATT_HEADS, ATT_DIM), lambda i, pt, sl: (i, 0, 0))
    grid_spec = pltpu.PrefetchScalarGridSpec(
        num_scalar_prefetch=2,
        grid=(b,),
        in_specs=[pl.BlockSpec((1,) + scores.shape[1:], lambda i, pt, sl: (i, 0, 0)),
                  row, row, row, pl.BlockSpec(memory_space=pl.ANY)],
        out_specs=row,
        scratch_shapes=[pltpu.VMEM((2, ATT_HEADS, ATT_PAGES, ATT_DIM, PAGE_SIZE), F32),
                        pltpu.SemaphoreType.DMA((2,))],
    )
    return pl.pallas_call(
        _moba_attend_kernel,
        grid_spec=grid_spec,
        out_shape=jax.ShapeDtypeStruct((b, ATT_HEADS, ATT_DIM), F32),
        compiler_params=pltpu.CompilerParams(dimension_semantics=("arbitrary",), vmem_limit_bytes=VMEM_LIMIT),
        name="moba_attend",
    )(page_table, sel, scores, q3, kn3, vn3, cache_vt)


FFN_COLS = 256


def _post_kernel(x_ref, ohg_ref, oatt_ref, prev_ref, wo_ref, gpm_ref, gpf_ref, gqf_ref, wi_ref, cw_ref, cb_ref,
                 wf_ref, y_ref, conv_ref, gate_ref, *, decode):
    li = pl.program_id(1)
    tl = x_ref.shape[1]
    d_ff = wf_ref.shape[0]
    mix = _dot(ohg_ref[0].astype(BF16), wo_ref[:HG_WIDTH, :]) + _dot(oatt_ref[0].astype(BF16), wo_ref[HG_WIDTH:, :])
    x1 = x_ref[0] + _rms(mix, gpm_ref[...])
    h2 = _rms(x1, gpf_ref[...]).astype(BF16)

    if not decode:
        @pl.when(li == 0)
        def _():
            gate_ref[0:SUBLANES, :] = jnp.zeros((SUBLANES, d_ff), F32)
            gate_ref[SUBLANES - 2:SUBLANES - 1, :] = prev_ref[0, 0]
            gate_ref[SUBLANES - 1:SUBLANES, :] = prev_ref[0, 1]

    acc = jnp.zeros((tl, x_ref.shape[2]), F32)
    for c0 in range(0, d_ff, FFN_COLS):
        cols = slice(c0, c0 + FFN_COLS)
        gate = _dot(h2, wi_ref[:, cols])
        up = _dot(h2, wi_ref[:, d_ff + c0:d_ff + c0 + FFN_COLS])
        w0 = cw_ref[0:1, cols]
        w1 = cw_ref[1:2, cols]
        w2 = cw_ref[2:3, cols]
        if decode:
            conv = prev_ref[0, 0, :, cols] * w0 + prev_ref[0, 1, :, cols] * w1 + gate * w2 + cb_ref[:, cols]
            conv_ref[0, 0, :, cols] = prev_ref[0, 1, :, cols]
            conv_ref[0, 1, :, cols] = gate
        else:
            gate_ref[SUBLANES:SUBLANES + tl, cols] = gate
            conv = (gate_ref[SUBLANES - 2:SUBLANES - 2 + tl, cols] * w0
                    + gate_ref[SUBLANES - 1:SUBLANES - 1 + tl, cols] * w1 + gate * w2 + cb_ref[:, cols])
        act = jax.nn.gelu(conv, approximate=True) * up
        acc = acc + _dot(act.astype(BF16), wf_ref[cols, :])
    y_ref[0] = x1 + _rms(acc, gqf_ref[...])

    if not decode:
        last = gate_ref[tl:tl + SUBLANES, :]
        gate_ref[0:SUBLANES, :] = last

        @pl.when(li == pl.num_programs(1) - 1)
        def _():
            conv_ref[0, 0] = last[SUBLANES - 2:SUBLANES - 1, :]
            conv_ref[0, 1] = last[SUBLANES - 1:SUBLANES, :]


def _post(x3, ohg3, oatt3, prev, wo, gpm, gpf, gqf, wi, cw, cb, wf, tl, decode):
    b, l, d = x3.shape
    d_ff = wf.shape[0]
    row = lambda width: pl.BlockSpec((1, tl, width), lambda i, j: (i, j, 0))
    const = lambda shape: pl.BlockSpec(shape, lambda i, j: (0,) * len(shape), pipeline_mode=pl.Buffered(1))
    if decode:
        state = pl.BlockSpec((1, CONV_WIDTH - 1, tl, d_ff), lambda i, j: (i, 0, j, 0))
    else:
        state = pl.BlockSpec((1, CONV_WIDTH - 1, 1, d_ff), lambda i, j: (i, 0, 0, 0))
    return pl.pallas_call(
        functools.partial(_post_kernel, decode=decode),
        grid=(b, l // tl),
        in_specs=[row(d), row(HG_WIDTH), row(ATT_WIDTH), state,
                  const(wo.shape), const((1, d)), const((1, d)), const((1, d)),
                  const(wi.shape), const(cw.shape), const((1, d_ff)), const(wf.shape)],
        out_specs=[row(d), state],
        out_shape=[jax.ShapeDtypeStruct((b, l, d), F32),
                   jax.ShapeDtypeStruct(prev.shape, F32)],
        scratch_shapes=[pltpu.VMEM((tl + SUBLANES, d_ff), F32)],
        compiler_params=pltpu.CompilerParams(dimension_semantics=("arbitrary", "arbitrary"),
                                             vmem_limit_bytes=VMEM_LIMIT),
        name="post_decode" if decode else "post_prompt",
    )(x3, ohg3, oatt3, prev, wo, gpm, gpf, gqf, wi, cw, cb, wf)


INPROJ_ROWS = 512
POST_ROWS = 512
HGRN_CHUNK = 128


def kernel(x_prompt, x_sample, cache_k, cache_v, state_hgrn, state_conv, page_table, w_in, w_out, hg_lb_logits,
           hg_norm, g_pre_mix, g_post_mix, g_pre_ffn, g_post_ffn, w_ffn_in, conv_w, conv_b, w_ffn_out):
    depth = w_in.shape[0]
    assert depth == 1, "single-layer step"
    bp, seq, d = x_prompt.shape
    bs, t_dec, _ = x_sample.shape
    assert t_dec == 1
    n_pages = page_table.shape[1]
    past = n_pages * PAGE_SIZE
    lb_all = jnp.cumsum(jax.nn.softmax(hg_lb_logits.astype(F32), axis=0), axis=0)

    layer = 0
    lb = lb_all[layer][None, :]
    ng = hg_norm[layer][None, :]
    w_in_b = w_in[layer].astype(BF16)
    w_out_b = w_out[layer].astype(BF16)
    w_fi_b = w_ffn_in[layer].astype(BF16)
    w_fo_b = w_ffn_out[layer].astype(BF16)
    gains = [g[layer][None, :] for g in (g_pre_mix, g_post_mix, g_pre_ffn, g_post_ffn)]
    cw = conv_w[layer]
    cb = conv_b[layer][None, :]
    post_w = (w_out_b, gains[1], gains[2], gains[3], w_fi_b, cw, cb, w_fo_b)

    tm = min(INPROJ_ROWS, seq)
    tabs_p = _rope_tables(jnp.arange(seq, dtype=I32))
    zh, q, k, v = _inproj(x_prompt.reshape(bp * seq, d), gains[0], w_in_b, tabs_p, tm)
    o_hg, s_p = _hgrn_prompt(zh.reshape(bp, seq, 4 * HG_WIDTH), lb, ng, min(HGRN_CHUNK, seq))
    q3, k3, v3 = (a.reshape(bp, seq, ATT_WIDTH) for a in (q, k, v))
    o_att = _moba_prompt(q3, k3, v3)
    d_ff = w_ffn_out.shape[1]
    zero_prev = jnp.zeros((bp, CONV_WIDTH - 1, 1, d_ff), F32)
    y_p, conv_p = _post(x_prompt, o_hg, o_att, zero_prev, *post_w, tl=min(POST_ROWS, seq), decode=False)
    conv_p = conv_p.reshape(bp, CONV_WIDTH - 1, d_ff)

    tabs_s = _rope_tables(jnp.full((bs,), past, dtype=I32))
    zh_s, q_s, k_s, v_s = _inproj(x_sample.reshape(bs, d), gains[0], w_in_b, tabs_s, bs)
    o_hg_s, s_s = _hgrn_step(zh_s.reshape(bs, 1, 4 * HG_WIDTH), lb, ng, state_hgrn[layer])
    heads = lambda a: a.reshape(bs, ATT_HEADS, ATT_DIM)
    cache_kt = jnp.transpose(cache_k, (0, 1, 3, 4, 2))
    cache_vt = jnp.transpose(cache_v, (0, 1, 3, 4, 2))
    scores, sel = _moba_scores(page_table, heads(q_s), cache_kt)
    o_att_s = _moba_attend(page_table, sel, scores, heads(q_s), heads(k_s), heads(v_s), cache_vt)
    rows3 = lambda a: a.reshape(1, bs, a.shape[-1])
    prev_s = jnp.transpose(state_conv[layer], (1, 0, 2))[None]
    y_s, conv_s = _post(rows3(x_sample), rows3(o_hg_s), o_att_s.reshape(1, bs, ATT_WIDTH), prev_s, *post_w,
                        tl=bs, decode=True)
    y_s = y_s.reshape(bs, 1, d)
    conv_s = jnp.transpose(conv_s[0], (1, 0, 2))

    kv_p = lambda a: a.reshape(1, bp, seq, ATT_HEADS, ATT_DIM)
    kv_s = lambda a: a.reshape(1, bs, 1, ATT_HEADS, ATT_DIM)
    return (y_p, y_s, kv_p(k), kv_p(v), s_p[None], conv_p[None],
            kv_s(k_s), kv_s(v_s), s_s[None], conv_s[None])
```

```python
import functools

import jax
import jax.numpy as jnp
from jax import lax
from jax.experimental import pallas as pl
from jax.experimental.pallas import tpu as pltpu

F32 = jnp.float32
BF16 = jnp.bfloat16
I32 = jnp.int32

HG_HEADS = 4
HG_DIM = 128
HG_WIDTH = HG_HEADS * HG_DIM
ATT_HEADS = 8
ATT_DIM = 64
ATT_WIDTH = ATT_HEADS * ATT_DIM
MOBA_BLOCK = 256
MOBA_TOPK = 3
ROPE_THETA = 500000.0
ROPE_DIMS = ATT_DIM // 4
ROPE_HALF = ROPE_DIMS // 2
CONV_WIDTH = 3
NORM_EPS = 1e-6
PAGE_SIZE = 128
PAGES_PER_BLOCK = MOBA_BLOCK // PAGE_SIZE

LANES = 128
SUBLANES = 8
VMEM_LIMIT = 56 * 1024 * 1024

NEG_INF = float("-inf")
_NT = (((1,), (1,)), ((), ()))
_TN = (((0,), (0,)), ((), ()))


def _dot(a, b):
    return jnp.dot(a, b, preferred_element_type=F32)


def _rms(x, g):
    return x * lax.rsqrt(jnp.mean(x * x, axis=-1, keepdims=True) + NORM_EPS) * g


def _sigmoid(x):
    return 1.0 / (1.0 + jnp.exp(-x))


def _split3(x):
    x1 = x.astype(BF16)
    r1 = x - x1.astype(F32)
    x2 = r1.astype(BF16)
    r2 = r1 - x2.astype(F32)
    return x1, x2, r2.astype(BF16)


def _dot_f32(a, b, dims):
    a1, a2, a3 = _split3(a)
    b1, b2, b3 = _split3(b)
    dg = functools.partial(lax.dot_general, dimension_numbers=dims, preferred_element_type=F32)
    small = dg(a1, b3) + dg(a3, b1) + dg(a2, b2)
    mid = dg(a1, b2) + dg(a2, b1)
    return dg(a1, b1) + (mid + small)


def _inproj_kernel(x_ref, g_ref, w_ref, cos_ref, sa_ref, sb_ref, zh_ref, q_ref, k_ref, v_ref):
    h = _rms(x_ref[...], g_ref[...]).astype(BF16)
    hgw = 4 * HG_WIDTH
    for c in range(4):
        zh_ref[:, c * HG_WIDTH:(c + 1) * HG_WIDTH] = _dot(h, w_ref[:, c * HG_WIDTH:(c + 1) * HG_WIDTH])
    cos = cos_ref[...]
    sa = sa_ref[...]
    sb = sb_ref[...]
    for out_ref, off in ((q_ref, hgw), (k_ref, hgw + ATT_WIDTH)):
        a = _dot(h, w_ref[:, off:off + ATT_WIDTH])
        for j in range(ATT_WIDTH // LANES):
            aj = a[:, j * LANES:(j + 1) * LANES]
            out_ref[:, j * LANES:(j + 1) * LANES] = (
                aj * cos + pltpu.roll(aj, LANES - ROPE_HALF, 1) * sa + pltpu.roll(aj, ROPE_HALF, 1) * sb)
    v_ref[...] = _dot(h, w_ref[:, hgw + 2 * ATT_WIDTH:hgw + 3 * ATT_WIDTH])


def _rope_tables(pos):
    inv = jnp.power(jnp.float32(ROPE_THETA), -jnp.arange(ROPE_HALF, dtype=F32) * (2.0 / ROPE_DIMS))
    ang = pos.astype(F32)[:, None] * inv[None, :]
    cos = jnp.cos(ang)
    sin = jnp.sin(ang)
    n = pos.shape[0]
    rest = ATT_DIM - ROPE_DIMS
    cos_h = jnp.concatenate([cos, cos, jnp.ones((n, rest), F32)], axis=1)
    sa_h = jnp.concatenate([-sin, jnp.zeros((n, ATT_DIM - ROPE_HALF), F32)], axis=1)
    sb_h = jnp.concatenate([jnp.zeros((n, ROPE_HALF), F32), sin, jnp.zeros((n, rest), F32)], axis=1)
    rep = LANES // ATT_DIM
    return tuple(jnp.tile(t, (1, rep)) for t in (cos_h, sa_h, sb_h))


def _rope_tables_t(pos):
    inv = jnp.power(jnp.float32(ROPE_THETA), -jnp.arange(ROPE_HALF, dtype=F32) * (2.0 / ROPE_DIMS))
    ang = pos.astype(F32)[:, None] * inv[None, :]
    return jnp.cos(ang).T, jnp.sin(ang).T


def _inproj(x2d, g, w_bf16, tables, tm):
    rows, d = x2d.shape
    ncols = w_bf16.shape[1]
    npos = tables[0].shape[0] // tm
    tab_spec = pl.BlockSpec((tm, LANES), lambda i: (i % npos, 0))
    row_spec = lambda width: pl.BlockSpec((tm, width), lambda i: (i, 0))
    return pl.pallas_call(
        _inproj_kernel,
        grid=(rows // tm,),
        in_specs=[row_spec(d),
                  pl.BlockSpec((1, d), lambda i: (0, 0)),
                  pl.BlockSpec((d, ncols), lambda i: (0, 0), pipeline_mode=pl.Buffered(1)),
                  tab_spec, tab_spec, tab_spec],
        out_specs=[row_spec(4 * HG_WIDTH), row_spec(ATT_WIDTH), row_spec(ATT_WIDTH), row_spec(ATT_WIDTH)],
        out_shape=[jax.ShapeDtypeStruct((rows, 4 * HG_WIDTH), F32),
                   jax.ShapeDtypeStruct((rows, ATT_WIDTH), F32),
                   jax.ShapeDtypeStruct((rows, ATT_WIDTH), F32),
                   jax.ShapeDtypeStruct((rows, ATT_WIDTH), F32)],
        compiler_params=pltpu.CompilerParams(dimension_semantics=("arbitrary",), vmem_limit_bytes=VMEM_LIMIT),
        name="inproj",
    )(x2d, g, w_bf16, *tables)


def _inproj_t_kernel(x_ref, g_ref, wh_ref, wat_ref, cos_ref, sin_ref, zh_ref, qt_ref, kt_ref, vt_ref):
    h = _rms(x_ref[...], g_ref[...]).astype(BF16)
    for c in range(4):
        zh_ref[:, c * HG_WIDTH:(c + 1) * HG_WIDTH] = _dot(h, wh_ref[:, c * HG_WIDTH:(c + 1) * HG_WIDTH])
    cos = cos_ref[...]
    sin = sin_ref[...]
    for idx, out_ref in enumerate((qt_ref, kt_ref, vt_ref)):
        a = lax.dot_general(wat_ref[idx * ATT_WIDTH:(idx + 1) * ATT_WIDTH, :], h, _NT,
                            preferred_element_type=F32)
        out_ref[0] = a
        if out_ref is vt_ref:
            continue
        for hd in range(ATT_HEADS):
            r = hd * ATT_DIM
            x1 = a[r:r + ROPE_HALF]
            x2 = a[r + ROPE_HALF:r + ROPE_DIMS]
            out_ref[0, r:r + ROPE_HALF, :] = x1 * cos - x2 * sin
            out_ref[0, r + ROPE_HALF:r + ROPE_DIMS, :] = x2 * cos + x1 * sin


def _inproj_t(x3, g, wh_bf16, wat_bf16, cos_t, sin_t, tm):
    b, l, d = x3.shape
    npos = l // tm
    x2d = x3.reshape(b * l, d)
    tab = pl.BlockSpec((ROPE_HALF, tm), lambda i: (0, i % npos))
    att = pl.BlockSpec((1, ATT_WIDTH, tm), lambda i: (i // npos, 0, i % npos))
    const = lambda shape: pl.BlockSpec(shape, lambda i: (0, 0), pipeline_mode=pl.Buffered(1))
    att_shape = jax.ShapeDtypeStruct((b, ATT_WIDTH, l), F32)
    return pl.pallas_call(
        _inproj_t_kernel,
        grid=(b * npos,),
        in_specs=[pl.BlockSpec((tm, d), lambda i: (i, 0)), const((1, d)), const(wh_bf16.shape),
                  const(wat_bf16.shape), tab, tab],
        out_specs=[pl.BlockSpec((tm, 4 * HG_WIDTH), lambda i: (i, 0)), att, att, att],
        out_shape=[jax.ShapeDtypeStruct((b * l, 4 * HG_WIDTH), F32), att_shape, att_shape, att_shape],
        compiler_params=pltpu.CompilerParams(dimension_semantics=("arbitrary",), vmem_limit_bytes=VMEM_LIMIT),
        name="inproj_t",
    )(x2d, g, wh_bf16, wat_bf16, cos_t, sin_t)


def _hgrn_gates(zh_row, lb, h):
    sl = lambda part: slice(part * HG_WIDTH + h * HG_DIM, part * HG_WIDTH + (h + 1) * HG_DIM)
    hq = zh_row[:, sl(0)]
    hf = zh_row[:, sl(1)]
    v = zh_row[:, sl(2)]
    hg = zh_row[:, sl(3)]
    q = hq * _sigmoid(hq)
    f = lb + (1.0 - lb) * _sigmoid(hf)
    return q, f, 1.0 - f, v, hg * _sigmoid(hg)


def _level_reference(b, b3, sub, level, c):
    m = 1 << level
    if m >= SUBLANES:
        pieces = []
        for grp in range(c // (2 * m)):
            r = grp * 2 * m + m - 1
            pieces.append(jnp.broadcast_to(b[r:r + 1, :], (2 * m, HG_DIM)))
        return pieces[0] if len(pieces) == 1 else jnp.concatenate(pieces, axis=0)
    beta3 = None
    for g0 in range(0, SUBLANES, 2 * m):
        r = g0 + m - 1
        piece = jnp.broadcast_to(b3[:, r:r + 1, :], b3.shape)
        beta3 = piece if beta3 is None else jnp.where(sub >= g0, piece, beta3)
    return beta3.reshape(c, HG_DIM)


def _hgrn_prompt_kernel(zh_ref, lb_ref, ng_ref, o_ref, s_out_ref, st_ref, lv_ref, *, chunk):
    c = chunk
    ci = pl.program_id(1)
    nlev = c.bit_length() - 1
    rows = lax.broadcasted_iota(I32, (c, c), 0)
    cols = lax.broadcasted_iota(I32, (c, c), 1)

    @pl.when(ci == 0)
    def _():
        st_ref[...] = jnp.zeros_like(st_ref)
        x = rows ^ cols
        lev = jnp.zeros((c, c), I32)
        for j in range(1, nlev):
            lev = lev + jnp.where(x >= (1 << j), 1, 0)
        lv_ref[...] = jnp.where(cols < rows, lev, jnp.where(cols == rows, -1, -2))

    lv = lv_ref[...]
    tri = jnp.where(cols <= rows, 1.0, 0.0).astype(BF16)
    sub = lax.broadcasted_iota(I32, (c // SUBLANES, SUBLANES, HG_DIM), 1)
    zh = zh_ref[0]
    ng = ng_ref[...]
    for h in range(HG_HEADS):
        lb = lb_ref[:, h * HG_DIM:(h + 1) * HG_DIM]
        q, f, k, v, og = _hgrn_gates(zh, lb, h)
        g1, g2, g3 = _split3(jnp.log(f))
        b = _dot(tri, g1) + (_dot(tri, g2) + _dot(tri, g3))
        b3 = b.reshape(c // SUBLANES, SUBLANES, HG_DIM)
        vb = v.astype(BF16)

        a = jnp.where(lv == -1, jnp.sum(q * k, axis=-1, keepdims=True), 0.0)
        for level in range(nlev):
            w = jnp.exp(-jnp.abs(b - _level_reference(b, b3, sub, level, c)))
            a_l = lax.dot_general((q * w).astype(BF16), (k * w).astype(BF16), _NT, preferred_element_type=F32)
            a = jnp.where(lv == level, a_l, a)

        st = st_ref[h]
        o = _dot(a.astype(BF16), vb) + lax.dot_general(
            (q * jnp.exp(b)).astype(BF16), st.astype(BF16), _NT, preferred_element_type=F32)
        b_end = b[c - 1:c, :]
        k_end = (k * jnp.exp(b_end - b)).astype(BF16)
        st_new = st * jnp.exp(b_end) + lax.dot_general(vb, k_end, _TN, preferred_element_type=F32)
        st_ref[h] = st_new
        o_ref[0, :, h * HG_DIM:(h + 1) * HG_DIM] = _rms(o, ng) * og

    @pl.when(ci == pl.num_programs(1) - 1)
    def _():
        for h in range(HG_HEADS):
            s_out_ref[0, h] = st_ref[h].T


def _hgrn_prompt(zh3, lb, ng, chunk):
    b, l, _ = zh3.shape
    return pl.pallas_call(
        functools.partial(_hgrn_prompt_kernel, chunk=chunk),
        grid=(b, l // chunk),
        in_specs=[pl.BlockSpec((1, chunk, 4 * HG_WIDTH), lambda i, j: (i, j, 0)),
                  pl.BlockSpec((1, HG_WIDTH), lambda i, j: (0, 0)),
                  pl.BlockSpec((1, HG_DIM), lambda i, j: (0, 0))],
        out_specs=[pl.BlockSpec((1, chunk, HG_WIDTH), lambda i, j: (i, j, 0)),
                   pl.BlockSpec((1, HG_HEADS, HG_DIM, HG_DIM), lambda i, j: (i, 0, 0, 0))],
        out_shape=[jax.ShapeDtypeStruct((b, l, HG_WIDTH), F32),
                   jax.ShapeDtypeStruct((b, HG_HEADS, HG_DIM, HG_DIM), F32)],
        scratch_shapes=[pltpu.VMEM((HG_HEADS, HG_DIM, HG_DIM), F32), pltpu.VMEM((chunk, chunk), I32)],
        compiler_params=pltpu.CompilerParams(dimension_semantics=("arbitrary", "arbitrary"),
                                             vmem_limit_bytes=VMEM_LIMIT),
        name="hgrn_prompt",
    )(zh3, lb, ng)


def _hgrn_step_kernel(zh_ref, lb_ref, ng_ref, s_ref, o_ref, s_out_ref):
    zh = zh_ref[0]
    ng = ng_ref[...]
    eye = (lax.broadcasted_iota(I32, (HG_DIM, HG_DIM), 0) == lax.broadcasted_iota(I32, (HG_DIM, HG_DIM), 1))

    def column(row):
        return jnp.sum(jnp.where(eye, row, 0.0), axis=1, keepdims=True)

    for h in range(HG_HEADS):
        lb = lb_ref[:, h * HG_DIM:(h + 1) * HG_DIM]
        q, f, k, v, og = _hgrn_gates(zh, lb, h)
        s_new = column(f) * s_ref[0, h] + column(k) * v
        s_out_ref[0, h] = s_new
        o = jnp.sum(column(q) * s_new, axis=0, keepdims=True)
        o_ref[0, :, h * HG_DIM:(h + 1) * HG_DIM] = _rms(o, ng) * og


def _hgrn_step(zh3, lb, ng, state):
    b = zh3.shape[0]
    return pl.pallas_call(
        _hgrn_step_kernel,
        grid=(b,),
        in_specs=[pl.BlockSpec((1, 1, 4 * HG_WIDTH), lambda i: (i, 0, 0)),
                  pl.BlockSpec((1, HG_WIDTH), lambda i: (0, 0)),
                  pl.BlockSpec((1, HG_DIM), lambda i: (0, 0)),
                  pl.BlockSpec((1, HG_HEADS, HG_DIM, HG_DIM), lambda i: (i, 0, 0, 0))],
        out_specs=[pl.BlockSpec((1, 1, HG_WIDTH), lambda i: (i, 0, 0)),
                   pl.BlockSpec((1, HG_HEADS, HG_DIM, HG_DIM), lambda i: (i, 0, 0, 0))],
        out_shape=[jax.ShapeDtypeStruct((b, 1, HG_WIDTH), F32),
                   jax.ShapeDtypeStruct(state.shape, F32)],
        compiler_params=pltpu.CompilerParams(dimension_semantics=("arbitrary",)),
        name="hgrn_step",
    )(zh3, lb, ng, state)


def _topk_rank(gm, axis, n_candidates, idx):
    rank = jnp.zeros(gm.shape, F32)
    for m in range(n_candidates):
        other = lax.slice_in_dim(gm, m, m + 1, axis=axis)
        tie = jnp.where(other == gm, jnp.where(idx > m, 1.0, 0.0), 0.0)
        rank = rank + jnp.where(other > gm, 1.0, tie)
    return rank


MOBA_Q_TILE = 128
MOBA_HEADS_PER_STEP = 2


def _moba_prompt_kernel(qt_ref, kt_ref, vt_ref, ot_ref):
    l = qt_ref.shape[2]
    nb = l // MOBA_BLOCK
    tq = MOBA_Q_TILE
    scale = ATT_DIM ** -0.5
    blk_row = lax.broadcasted_iota(I32, (SUBLANES, l), 0)
    own_blk = lax.broadcasted_iota(I32, (SUBLANES, l), 1) // MOBA_BLOCK
    causal = {}
    for t0 in range(0, MOBA_BLOCK, tq):
        shape = (t0 + tq, tq)
        causal[t0] = lax.broadcasted_iota(I32, shape, 0) <= lax.broadcasted_iota(I32, shape, 1) + t0
    for hh in range(MOBA_HEADS_PER_STEP):
        chans = slice(hh * ATT_DIM, (hh + 1) * ATT_DIM)
        qt = qt_ref[0, chans, :]
        k_rows = kt_ref[0, chans, :].T
        vtb = vt_ref[0, chans, :].astype(BF16)
        kmean = jnp.sum(k_rows.reshape(nb, MOBA_BLOCK, ATT_DIM), axis=1) * (1.0 / MOBA_BLOCK)
        if nb < SUBLANES:
            kmean = jnp.concatenate([kmean, jnp.zeros((SUBLANES - nb, ATT_DIM), F32)], axis=0)
        gate = _dot_f32(kmean, qt, (((1,), (0,)), ((), ())))
        valid = blk_row < own_blk
        rank = _topk_rank(jnp.where(valid, gate, NEG_INF), 0, nb, blk_row)
        sel = jnp.where(valid, jnp.where(rank < MOBA_TOPK, 1.0, 0.0), 0.0)
        qtb = (qt * scale).astype(BF16)
        kb = k_rows.astype(BF16)
        for i in range(nb):
            for t0 in range(0, MOBA_BLOCK, tq):
                q0 = i * MOBA_BLOCK + t0
                q_t = qtb[:, q0:q0 + tq]
                n_keys = i * MOBA_BLOCK + t0 + tq
                s = _dot(kb[:n_keys], q_t)
                pieces = [jnp.where(sel[j:j + 1, q0:q0 + tq] > 0.5, s[j * MOBA_BLOCK:(j + 1) * MOBA_BLOCK], NEG_INF)
                          for j in range(i)]
                pieces.append(jnp.where(causal[t0], s[i * MOBA_BLOCK:], NEG_INF))
                sm = pieces[0] if i == 0 else jnp.concatenate(pieces, axis=0)
                p = jnp.exp(sm - jnp.max(sm, axis=0, keepdims=True))
                den = jnp.sum(p, axis=0, keepdims=True)
                ot_ref[0, chans, q0:q0 + tq] = _dot(vtb[:, :n_keys], p.astype(BF16)) / den


def _moba_prompt(qt3, kt3, vt3):
    b, _, l = qt3.shape
    assert l % MOBA_BLOCK == 0 and l // MOBA_BLOCK <= SUBLANES
    width = MOBA_HEADS_PER_STEP * ATT_DIM
    spec = pl.BlockSpec((1, width, l), lambda i, j: (i, j, 0))
    return pl.pallas_call(
        _moba_prompt_kernel,
        grid=(b, ATT_WIDTH // width),
        in_specs=[spec, spec, spec],
        out_specs=spec,
        out_shape=jax.ShapeDtypeStruct((b, ATT_WIDTH, l), F32),
        compiler_params=pltpu.CompilerParams(dimension_semantics=("arbitrary", "arbitrary"),
                                             vmem_limit_bytes=VMEM_LIMIT),
        name="moba_prompt",
    )(qt3, kt3, vt3)


SEL_PAGES = 16


def _column(row, eye):
    return jnp.sum(jnp.where(eye, row, 0.0), axis=1, keepdims=True)


def _moba_scores_kernel(pt_ref, q_ref, kc_ref, s_ref, sel_ref, buf_ref, sem_ref, *, n_pages):
    bi = pl.program_id(0)
    ci = pl.program_id(1)
    nch = pl.num_programs(1)
    step = bi * nch + ci
    slot = step % 2
    nblk = n_pages // PAGES_PER_BLOCK

    def copies(b_idx, c_idx, slot_idx):
        return [pltpu.make_async_copy(kc_ref.at[0, pt_ref[b_idx, c_idx * SEL_PAGES + p]],
                                      buf_ref.at[slot_idx, p], sem_ref.at[slot_idx])
                for p in range(SEL_PAGES)]

    @pl.when(step == 0)
    def _():
        for cp in copies(bi, ci, slot):
            cp.start()

    @pl.when(step + 1 < pl.num_programs(0) * nch)
    def _():
        nxt = step + 1
        for cp in copies(nxt // nch, nxt % nch, 1 - slot):
            cp.start()

    for cp in copies(bi, ci, slot):
        cp.wait()

    eye = (lax.broadcasted_iota(I32, (ATT_DIM, ATT_DIM), 0) == lax.broadcasted_iota(I32, (ATT_DIM, ATT_DIM), 1))
    q = q_ref[0]
    q_cols = [_column(q[h:h + 1, :], eye) for h in range(ATT_HEADS)]
    for p in range(SEL_PAGES):
        rows = [jnp.sum(buf_ref[slot, p, h] * q_cols[h], axis=0, keepdims=True) for h in range(ATT_HEADS)]
        blk = (ci * SEL_PAGES + p) // PAGES_PER_BLOCK
        half = p % PAGES_PER_BLOCK
        s_ref[0, pl.ds(pl.multiple_of(blk * ATT_HEADS, ATT_HEADS), ATT_HEADS),
              half * PAGE_SIZE:(half + 1) * PAGE_SIZE] = jnp.concatenate(rows, axis=0)

    @pl.when(ci == nch - 1)
    def _():
        lane = lax.broadcasted_iota(I32, (ATT_HEADS, LANES), 1)
        gate = jnp.full((ATT_HEADS, LANES), NEG_INF, F32)
        for n in range(nblk):
            g_n = jnp.sum(s_ref[0, n * ATT_HEADS:(n + 1) * ATT_HEADS, :], axis=1, keepdims=True) * (1.0 / MOBA_BLOCK)
            gate = jnp.where(lane == n, g_n, gate)
        rank = _topk_rank(gate, 1, nblk, lane)
        lane_f = lane.astype(F32)
        for j in range(MOBA_TOPK):
            pick = jnp.sum(jnp.where(rank == float(j), jnp.where(lane < nblk, lane_f, 0.0), 0.0),
                           axis=1, keepdims=True)
            sel_ref[0, j] = pick.astype(I32)


def _moba_scores(page_table, q3, cache_kt):
    b, n_pages = page_table.shape
    nch = n_pages // SEL_PAGES
    nblk = n_pages // PAGES_PER_BLOCK
    grid_spec = pltpu.PrefetchScalarGridSpec(
        num_scalar_prefetch=1,
        grid=(b, nch),
        in_specs=[pl.BlockSpec((1, ATT_HEADS, ATT_DIM), lambda i, j, pt: (i, 0, 0)),
                  pl.BlockSpec(memory_space=pl.ANY)],
        out_specs=[pl.BlockSpec((1, nblk * ATT_HEADS, MOBA_BLOCK), lambda i, j, pt: (i, 0, 0)),
                   pl.BlockSpec((1, MOBA_TOPK, ATT_HEADS, 1), lambda i, j, pt: (i, 0, 0, 0))],
        scratch_shapes=[pltpu.VMEM((2, SEL_PAGES, ATT_HEADS, ATT_DIM, PAGE_SIZE), F32),
                        pltpu.SemaphoreType.DMA((2,))],
    )
    scores, sel = pl.pallas_call(
        functools.partial(_moba_scores_kernel, n_pages=n_pages),
        grid_spec=grid_spec,
        out_shape=[jax.ShapeDtypeStruct((b, nblk * ATT_HEADS, MOBA_BLOCK), F32),
                   jax.ShapeDtypeStruct((b, MOBA_TOPK, ATT_HEADS, 1), I32)],
        compiler_params=pltpu.CompilerParams(dimension_semantics=("arbitrary", "arbitrary"),
                                             vmem_limit_bytes=VMEM_LIMIT),
        name="moba_scores",
    )(page_table, q3, cache_kt)
    return scores, sel.reshape(b, MOBA_TOPK, ATT_HEADS)


ATT_PAGES = MOBA_TOPK * PAGES_PER_BLOCK


def _moba_attend_kernel(pt_ref, sel_ref, s_ref, q_ref, kn_ref, vn_ref, vc_ref, o_ref, vbuf, sem_ref):
    bi = pl.program_id(0)
    slot = bi % 2
    scale = ATT_DIM ** -0.5

    def copies(b_idx, slot_idx):
        out = []
        for h in range(ATT_HEADS):
            for j in range(MOBA_TOPK):
                blk = sel_ref[b_idx, j, h]
                for p in range(PAGES_PER_BLOCK):
                    page = pt_ref[b_idx, blk * PAGES_PER_BLOCK + p]
                    out.append(pltpu.make_async_copy(vc_ref.at[0, page, h],
                                                     vbuf.at[slot_idx, h, j * PAGES_PER_BLOCK + p],
                                                     sem_ref.at[slot_idx]))
        return out

    @pl.when(bi == 0)
    def _():
        for cp in copies(bi, slot):
            cp.start()

    @pl.when(bi + 1 < pl.num_programs(0))
    def _():
        for cp in copies(bi + 1, 1 - slot):
            cp.start()

    for cp in copies(bi, slot):
        cp.wait()

    eye = (lax.broadcasted_iota(I32, (ATT_DIM, LANES), 0) == lax.broadcasted_iota(I32, (ATT_DIM, LANES), 1))
    s_own_all = jnp.sum(q_ref[0] * kn_ref[0], axis=1, keepdims=True) * scale
    for h in range(ATT_HEADS):
        s_blk = [s_ref[0, pl.ds(sel_ref[bi, j, h] * ATT_HEADS + h, 1), :] * scale for j in range(MOBA_TOPK)]
        s_own = s_own_all[h:h + 1, :]
        mx = s_own
        for s_j in s_blk:
            mx = jnp.maximum(mx, jnp.max(s_j, axis=1, keepdims=True))
        p_blk = [jnp.exp(s_j - mx) for s_j in s_blk]
        p_own = jnp.exp(s_own - mx)
        den = p_own
        acc = jnp.zeros((ATT_DIM, PAGE_SIZE), F32)
        for j, p_j in enumerate(p_blk):
            den = den + jnp.sum(p_j, axis=1, keepdims=True)
            for p in range(PAGES_PER_BLOCK):
                acc = acc + p_j[:, p * PAGE_SIZE:(p + 1) * PAGE_SIZE] * vbuf[slot, h, j * PAGES_PER_BLOCK + p]
        o_col = jnp.sum(acc, axis=1, keepdims=True)
        o_row = jnp.sum(jnp.where(eye, o_col, 0.0), axis=0, keepdims=True)[:, :ATT_DIM]
        o_ref[0, h:h + 1, :] = (o_row + p_own * vn_ref[0, h:h + 1, :]) / den


def _moba_attend(page_table, sel, scores, q3, kn3, vn3, cache_vt):
    b = page_table.shape[0]
    row = pl.BlockSpec((1, ATT_HEADS, ATT_DIM), lambda i, pt, sl: (i, 0, 0))
    grid_spec = pltpu.PrefetchScalarGridSpec(
        num_scalar_prefetch=2,
        grid=(b,),
        in_specs=[pl.BlockSpec((1,) + scores.shape[1:], lambda i, pt, sl: (i, 0, 0)),
                  row, row, row, pl.BlockSpec(memory_space=pl.ANY)],
        out_specs=row,
        scratch_shapes=[pltpu.VMEM((2, ATT_HEADS, ATT_PAGES, ATT_DIM, PAGE_SIZE), F32),
                        pltpu.SemaphoreType.DMA((2,))],
    )
    return pl.pallas_call(
        _moba_attend_kernel,
        grid_spec=grid_spec,
        out_shape=jax.ShapeDtypeStruct((b, ATT_HEADS, ATT_DIM), F32),
        compiler_params=pltpu.CompilerParams(dimension_semantics=("arbitrary",), vmem_limit_bytes=VMEM_LIMIT),
        name="moba_attend",
    )(page_table, sel, scores, q3, kn3, vn3, cache_vt)


FFN_COLS = 256


def _post_kernel(x_ref, ohg_ref, oatt_ref, prev_ref, wo_ref, gpm_ref, gpf_ref, gqf_ref, wi_ref, cw_ref, cb_ref,
                 wf_ref, y_ref, conv_ref, gate_ref, *, decode):
    li = pl.program_id(1)
    tl = x_ref.shape[1]
    d_ff = wf_ref.shape[0]
    if decode:
        att = _dot(oatt_ref[0].astype(BF16), wo_ref[HG_WIDTH:, :])
    else:
        att = lax.dot_general(oatt_ref[0].astype(BF16), wo_ref[HG_WIDTH:, :], _TN, preferred_element_type=F32)
    mix = _dot(ohg_ref[0].astype(BF16), wo_ref[:HG_WIDTH, :]) + att
    x1 = x_ref[0] + _rms(mix, gpm_ref[...])
    h2 = _rms(x1, gpf_ref[...]).astype(BF16)

    if not decode:
        @pl.when(li == 0)
        def _():
            gate_ref[0:SUBLANES, :] = jnp.zeros((SUBLANES, d_ff), F32)
            gate_ref[SUBLANES - 2:SUBLANES - 1, :] = prev_ref[0, 0]
            gate_ref[SUBLANES - 1:SUBLANES, :] = prev_ref[0, 1]

    acc = jnp.zeros((tl, x_ref.shape[2]), F32)
    for c0 in range(0, d_ff, FFN_COLS):
        cols = slice(c0, c0 + FFN_COLS)
        gate = _dot(h2, wi_ref[:, cols])
        up = _dot(h2, wi_ref[:, d_ff + c0:d_ff + c0 + FFN_COLS])
        w0 = cw_ref[0:1, cols]
        w1 = cw_ref[1:2, cols]
        w2 = cw_ref[2:3, cols]
        if decode:
            conv = prev_ref[0, 0, :, cols] * w0 + prev_ref[0, 1, :, cols] * w1 + gate * w2 + cb_ref[:, cols]
            conv_ref[0, 0, :, cols] = prev_ref[0, 1, :, cols]
            conv_ref[0, 1, :, cols] = gate
        else:
            gate_ref[SUBLANES:SUBLANES + tl, cols] = gate
            conv = (gate_ref[SUBLANES - 2:SUBLANES - 2 + tl, cols] * w0
                    + gate_ref[SUBLANES - 1:SUBLANES - 1 + tl, cols] * w1 + gate * w2 + cb_ref[:, cols])
        act = jax.nn.gelu(conv, approximate=True) * up
        acc = acc + _dot(act.astype(BF16), wf_ref[cols, :])
    y_ref[0] = x1 + _rms(acc, gqf_ref[...])

    if not decode:
        last = gate_ref[tl:tl + SUBLANES, :]
        gate_ref[0:SUBLANES, :] = last

        @pl.when(li == pl.num_programs(1) - 1)
        def _():
            conv_ref[0, 0] = last[SUBLANES - 2:SUBLANES - 1, :]
            conv_ref[0, 1] = last[SUBLANES - 1:SUBLANES, :]


def _post(x3, ohg3, oatt3, prev, wo, gpm, gpf, gqf, wi, cw, cb, wf, tl, decode):
    b, l, d = x3.shape
    d_ff = wf.shape[0]
    row = lambda width: pl.BlockSpec((1, tl, width), lambda i, j: (i, j, 0))
    const = lambda shape: pl.BlockSpec(shape, lambda i, j: (0,) * len(shape), pipeline_mode=pl.Buffered(1))
    if decode:
        state = pl.BlockSpec((1, CONV_WIDTH - 1, tl, d_ff), lambda i, j: (i, 0, j, 0))
        att = row(ATT_WIDTH)
    else:
        state = pl.BlockSpec((1, CONV_WIDTH - 1, 1, d_ff), lambda i, j: (i, 0, 0, 0))
        att = pl.BlockSpec((1, ATT_WIDTH, tl), lambda i, j: (i, 0, j))
    return pl.pallas_call(
        functools.partial(_post_kernel, decode=decode),
        grid=(b, l // tl),
        in_specs=[row(d), row(HG_WIDTH), att, state,
                  const(wo.shape), const((1, d)), const((1, d)), const((1, d)),
                  const(wi.shape), const(cw.shape), const((1, d_ff)), const(wf.shape)],
        out_specs=[row(d), state],
        out_shape=[jax.ShapeDtypeStruct((b, l, d), F32),
                   jax.ShapeDtypeStruct(prev.shape, F32)],
        scratch_shapes=[pltpu.VMEM((tl + SUBLANES, d_ff), F32)],
        compiler_params=pltpu.CompilerParams(dimension_semantics=("arbitrary", "arbitrary"),
                                             vmem_limit_bytes=VMEM_LIMIT),
        name="post_decode" if decode else "post_prompt",
    )(x3, ohg3, oatt3, prev, wo, gpm, gpf, gqf, wi, cw, cb, wf)


INPROJ_ROWS = 512
POST_ROWS = 512
HGRN_CHUNK = 256


def kernel(x_prompt, x_sample, cache_k, cache_v, state_hgrn, state_conv, page_table, w_in, w_out, hg_lb_logits,
           hg_norm, g_pre_mix, g_post_mix, g_pre_ffn, g_post_ffn, w_ffn_in, conv_w, conv_b, w_ffn_out):
    depth = w_in.shape[0]
    assert depth == 1, "single-layer step"
    bp, seq, d = x_prompt.shape
    bs, t_dec, _ = x_sample.shape
    assert t_dec == 1
    n_pages = page_table.shape[1]
    past = n_pages * PAGE_SIZE
    lb_all = jnp.cumsum(jax.nn.softmax(hg_lb_logits.astype(F32), axis=0), axis=0)

    layer = 0
    lb = lb_all[layer][None, :]
    ng = hg_norm[layer][None, :]
    w_in_b = w_in[layer].astype(BF16)
    w_out_b = w_out[layer].astype(BF16)
    w_fi_b = w_ffn_in[layer].astype(BF16)
    w_fo_b = w_ffn_out[layer].astype(BF16)
    gains = [g[layer][None, :] for g in (g_pre_mix, g_post_mix, g_pre_ffn, g_post_ffn)]
    cw = conv_w[layer]
    cb = conv_b[layer][None, :]
    post_w = (w_out_b, gains[1], gains[2], gains[3], w_fi_b, cw, cb, w_fo_b)

    tm = min(INPROJ_ROWS, seq)
    cos_t, sin_t = _rope_tables_t(jnp.arange(seq, dtype=I32))
    w_h = w_in_b[:, :4 * HG_WIDTH]
    w_at = w_in_b[:, 4 * HG_WIDTH:].T
    zh, qt, kt, vt = _inproj_t(x_prompt, gains[0], w_h, w_at, cos_t, sin_t, tm)
    o_hg, s_p = _hgrn_prompt(zh.reshape(bp, seq, 4 * HG_WIDTH), lb, ng, min(HGRN_CHUNK, seq))
    o_att = _moba_prompt(qt, kt, vt)
    d_ff = w_ffn_out.shape[1]
    zero_prev = jnp.zeros((bp, CONV_WIDTH - 1, 1, d_ff), F32)
    y_p, conv_p = _post(x_prompt, o_hg, o_att, zero_prev, *post_w, tl=min(POST_ROWS, seq), decode=False)
    conv_p = conv_p.reshape(bp, CONV_WIDTH - 1, d_ff)

    tabs_s = _rope_tables(jnp.full((bs,), past, dtype=I32))
    zh_s, q_s, k_s, v_s = _inproj(x_sample.reshape(bs, d), gains[0], w_in_b, tabs_s, bs)
    o_hg_s, s_s = _hgrn_step(zh_s.reshape(bs, 1, 4 * HG_WIDTH), lb, ng, state_hgrn[layer])
    heads = lambda a: a.reshape(bs, ATT_HEADS, ATT_DIM)
    cache_kt = jnp.transpose(cache_k, (0, 1, 3, 4, 2))
    cache_vt = jnp.transpose(cache_v, (0, 1, 3, 4, 2))
    scores, sel = _moba_scores(page_table, heads(q_s), cache_kt)
    o_att_s = _moba_attend(page_table, sel, scores, heads(q_s), heads(k_s), heads(v_s), cache_vt)
    rows3 = lambda a: a.reshape(1, bs, a.shape[-1])
    prev_s = jnp.transpose(state_conv[layer], (1, 0, 2))[None]
    y_s, conv_s = _post(rows3(x_sample), rows3(o_hg_s), o_att_s.reshape(1, bs, ATT_WIDTH), prev_s, *post_w,
                        tl=bs, decode=True)
    y_s = y_s.reshape(bs, 1, d)
    conv_s = jnp.transpose(conv_s[0], (1, 0, 2))

    kv_p = lambda a: jnp.transpose(a.reshape(1, bp, ATT_HEADS, ATT_DIM, seq), (0, 1, 4, 2, 3))
    kv_s = lambda a: a.reshape(1, bs, 1, ATT_HEADS, ATT_DIM)
    return (y_p, y_s, kv_p(kt), kv_p(vt), s_p[None], conv_p[None],
            kv_s(k_s), kv_s(v_s), s_s[None], conv_s[None])
```

```python
import functools

import jax
import jax.numpy as jnp
from jax import lax
from jax.experimental import pallas as pl
from jax.experimental.pallas import tpu as pltpu

F32 = jnp.float32
BF16 = jnp.bfloat16
I32 = jnp.int32

HG_HEADS = 4
HG_DIM = 128
HG_WIDTH = HG_HEADS * HG_DIM
ATT_HEADS = 8
ATT_DIM = 64
ATT_WIDTH = ATT_HEADS * ATT_DIM
MOBA_BLOCK = 256
MOBA_TOPK = 3
ROPE_THETA = 500000.0
ROPE_DIMS = ATT_DIM // 4
ROPE_HALF = ROPE_DIMS // 2
CONV_WIDTH = 3
NORM_EPS = 1e-6
PAGE_SIZE = 128
PAGES_PER_BLOCK = MOBA_BLOCK // PAGE_SIZE

LANES = 128
SUBLANES = 8
VMEM_LIMIT = 56 * 1024 * 1024

NEG_INF = float("-inf")
LOG2_E = 1.4426950408889634
_NT = (((1,), (1,)), ((), ()))
_TN = (((0,), (0,)), ((), ()))


def _dot(a, b):
    return jnp.dot(a, b, preferred_element_type=F32)


def _rms(x, g):
    return x * lax.rsqrt(jnp.mean(x * x, axis=-1, keepdims=True) + NORM_EPS) * g


def _sigmoid(x):
    return 0.5 * jnp.tanh(0.5 * x) + 0.5


def _split3(x):
    x1 = x.astype(BF16)
    r1 = x - x1.astype(F32)
    x2 = r1.astype(BF16)
    r2 = r1 - x2.astype(F32)
    return x1, x2, r2.astype(BF16)


def _dot_f32(a, b, dims):
    a1, a2, a3 = _split3(a)
    b1, b2, b3 = _split3(b)
    dg = functools.partial(lax.dot_general, dimension_numbers=dims, preferred_element_type=F32)
    small = dg(a1, b3) + dg(a3, b1) + dg(a2, b2)
    mid = dg(a1, b2) + dg(a2, b1)
    return dg(a1, b1) + (mid + small)


def _inproj_kernel(x_ref, g_ref, w_ref, cos_ref, sa_ref, sb_ref, zh_ref, q_ref, k_ref, v_ref):
    h = _rms(x_ref[...], g_ref[...]).astype(BF16)
    hgw = 4 * HG_WIDTH
    for c in range(4):
        zh_ref[:, c * HG_WIDTH:(c + 1) * HG_WIDTH] = _dot(h, w_ref[:, c * HG_WIDTH:(c + 1) * HG_WIDTH])
    cos = cos_ref[...]
    sa = sa_ref[...]
    sb = sb_ref[...]
    for out_ref, off in ((q_ref, hgw), (k_ref, hgw + ATT_WIDTH)):
        a = _dot(h, w_ref[:, off:off + ATT_WIDTH])
        for j in range(ATT_WIDTH // LANES):
            aj = a[:, j * LANES:(j + 1) * LANES]
            out_ref[:, j * LANES:(j + 1) * LANES] = (
                aj * cos + pltpu.roll(aj, LANES - ROPE_HALF, 1) * sa + pltpu.roll(aj, ROPE_HALF, 1) * sb)
    v_ref[...] = _dot(h, w_ref[:, hgw + 2 * ATT_WIDTH:hgw + 3 * ATT_WIDTH])


def _rope_tables(pos):
    inv = jnp.power(jnp.float32(ROPE_THETA), -jnp.arange(ROPE_HALF, dtype=F32) * (2.0 / ROPE_DIMS))
    ang = pos.astype(F32)[:, None] * inv[None, :]
    cos = jnp.cos(ang)
    sin = jnp.sin(ang)
    n = pos.shape[0]
    rest = ATT_DIM - ROPE_DIMS
    cos_h = jnp.concatenate([cos, cos, jnp.ones((n, rest), F32)], axis=1)
    sa_h = jnp.concatenate([-sin, jnp.zeros((n, ATT_DIM - ROPE_HALF), F32)], axis=1)
    sb_h = jnp.concatenate([jnp.zeros((n, ROPE_HALF), F32), sin, jnp.zeros((n, rest), F32)], axis=1)
    rep = LANES // ATT_DIM
    return tuple(jnp.tile(t, (1, rep)) for t in (cos_h, sa_h, sb_h))


def _rope_tables_t(pos):
    inv = jnp.power(jnp.float32(ROPE_THETA), -jnp.arange(ROPE_HALF, dtype=F32) * (2.0 / ROPE_DIMS))
    ang = pos.astype(F32)[:, None] * inv[None, :]
    return jnp.cos(ang).T, jnp.sin(ang).T


def _inproj(x2d, g, w_bf16, tables, tm):
    rows, d = x2d.shape
    ncols = w_bf16.shape[1]
    npos = tables[0].shape[0] // tm
    tab_spec = pl.BlockSpec((tm, LANES), lambda i: (i % npos, 0))
    row_spec = lambda width: pl.BlockSpec((tm, width), lambda i: (i, 0))
    return pl.pallas_call(
        _inproj_kernel,
        grid=(rows // tm,),
        in_specs=[row_spec(d),
                  pl.BlockSpec((1, d), lambda i: (0, 0)),
                  pl.BlockSpec((d, ncols), lambda i: (0, 0), pipeline_mode=pl.Buffered(1)),
                  tab_spec, tab_spec, tab_spec],
        out_specs=[row_spec(4 * HG_WIDTH), row_spec(ATT_WIDTH), row_spec(ATT_WIDTH), row_spec(ATT_WIDTH)],
        out_shape=[jax.ShapeDtypeStruct((rows, 4 * HG_WIDTH), F32),
                   jax.ShapeDtypeStruct((rows, ATT_WIDTH), F32),
                   jax.ShapeDtypeStruct((rows, ATT_WIDTH), F32),
                   jax.ShapeDtypeStruct((rows, ATT_WIDTH), F32)],
        compiler_params=pltpu.CompilerParams(dimension_semantics=("arbitrary",), vmem_limit_bytes=VMEM_LIMIT),
        name="inproj",
    )(x2d, g, w_bf16, *tables)


def _inproj_t_kernel(x_ref, g_ref, wh_ref, wat_ref, cos_ref, sin_ref, zh_ref, qt_ref, kt_ref, vt_ref):
    h = _rms(x_ref[...], g_ref[...]).astype(BF16)
    for c in range(4):
        zh_ref[:, c * HG_WIDTH:(c + 1) * HG_WIDTH] = _dot(h, wh_ref[:, c * HG_WIDTH:(c + 1) * HG_WIDTH])
    cos = cos_ref[...]
    sin = sin_ref[...]
    for idx, out_ref in enumerate((qt_ref, kt_ref, vt_ref)):
        a = lax.dot_general(wat_ref[idx * ATT_WIDTH:(idx + 1) * ATT_WIDTH, :], h, _NT,
                            preferred_element_type=F32)
        out_ref[0] = a
        if out_ref is vt_ref:
            continue
        for hd in range(ATT_HEADS):
            r = hd * ATT_DIM
            x1 = a[r:r + ROPE_HALF]
            x2 = a[r + ROPE_HALF:r + ROPE_DIMS]
            out_ref[0, r:r + ROPE_HALF, :] = x1 * cos - x2 * sin
            out_ref[0, r + ROPE_HALF:r + ROPE_DIMS, :] = x2 * cos + x1 * sin


def _inproj_t(x3, g, wh_bf16, wat_bf16, cos_t, sin_t, tm):
    b, l, d = x3.shape
    npos = l // tm
    x2d = x3.reshape(b * l, d)
    tab = pl.BlockSpec((ROPE_HALF, tm), lambda i: (0, i % npos))
    att = pl.BlockSpec((1, ATT_WIDTH, tm), lambda i: (i // npos, 0, i % npos))
    const = lambda shape: pl.BlockSpec(shape, lambda i: (0, 0), pipeline_mode=pl.Buffered(1))
    att_shape = jax.ShapeDtypeStruct((b, ATT_WIDTH, l), F32)
    return pl.pallas_call(
        _inproj_t_kernel,
        grid=(b * npos,),
        in_specs=[pl.BlockSpec((tm, d), lambda i: (i, 0)), const((1, d)), const(wh_bf16.shape),
                  const(wat_bf16.shape), tab, tab],
        out_specs=[pl.BlockSpec((tm, 4 * HG_WIDTH), lambda i: (i, 0)), att, att, att],
        out_shape=[jax.ShapeDtypeStruct((b * l, 4 * HG_WIDTH), F32), att_shape, att_shape, att_shape],
        compiler_params=pltpu.CompilerParams(dimension_semantics=("arbitrary",), vmem_limit_bytes=VMEM_LIMIT),
        name="inproj_t",
    )(x2d, g, wh_bf16, wat_bf16, cos_t, sin_t)


def _hgrn_gates(zh_row, lb, h):
    sl = lambda part: slice(part * HG_WIDTH + h * HG_DIM, part * HG_WIDTH + (h + 1) * HG_DIM)
    hq = zh_row[:, sl(0)]
    hf = zh_row[:, sl(1)]
    v = zh_row[:, sl(2)]
    hg = zh_row[:, sl(3)]
    q = hq * _sigmoid(hq)
    f = lb + (1.0 - lb) * _sigmoid(hf)
    return q, f, 1.0 - f, v, hg * _sigmoid(hg)


def _level_reference(b, b3, sub, level, c):
    m = 1 << level
    if m >= SUBLANES:
        pieces = []
        for grp in range(c // (2 * m)):
            r = grp * 2 * m + m - 1
            pieces.append(jnp.broadcast_to(b[r:r + 1, :], (2 * m, HG_DIM)))
        return pieces[0] if len(pieces) == 1 else jnp.concatenate(pieces, axis=0)
    beta3 = None
    for g0 in range(0, SUBLANES, 2 * m):
        r = g0 + m - 1
        piece = jnp.broadcast_to(b3[:, r:r + 1, :], b3.shape)
        beta3 = piece if beta3 is None else jnp.where(sub >= g0, piece, beta3)
    return beta3.reshape(c, HG_DIM)


def _hgrn_prompt_kernel(zh_ref, lb_ref, ng_ref, o_ref, s_out_ref, st_ref, lv_ref, *, chunk):
    c = chunk
    ci = pl.program_id(1)
    nlev = c.bit_length() - 1
    rows = lax.broadcasted_iota(I32, (c, c), 0)
    cols = lax.broadcasted_iota(I32, (c, c), 1)

    @pl.when(ci == 0)
    def _():
        st_ref[...] = jnp.zeros_like(st_ref)
        x = rows ^ cols
        lev = jnp.zeros((c, c), I32)
        for j in range(1, nlev):
            lev = lev + jnp.where(x >= (1 << j), 1, 0)
        lv_ref[...] = jnp.where(cols < rows, lev, jnp.where(cols == rows, -1, -2))

    lv = lv_ref[...]
    tri = jnp.where(cols <= rows, 1.0, 0.0).astype(BF16)
    sub = lax.broadcasted_iota(I32, (c // SUBLANES, SUBLANES, HG_DIM), 1)
    zh = zh_ref[0]
    ng = ng_ref[...]
    heads = []
    for h in range(HG_HEADS):
        lb = lb_ref[:, h * HG_DIM:(h + 1) * HG_DIM]
        q, f, k, v, og = _hgrn_gates(zh, lb, h)
        g1, g2, g3 = _split3(jnp.log(f))
        b = (_dot(tri, g1) + (_dot(tri, g2) + _dot(tri, g3))) * LOG2_E
        heads.append(dict(q=q, k=k, og=og, b=b, b3=b.reshape(c // SUBLANES, SUBLANES, HG_DIM), vb=v.astype(BF16),
                          a=jnp.where(lv == -1, jnp.sum(q * k, axis=-1, keepdims=True), 0.0)))

    for level in range(nlev):
        for hd in heads:
            w = jnp.exp2(-jnp.abs(hd["b"] - _level_reference(hd["b"], hd["b3"], sub, level, c)))
            a_l = lax.dot_general((hd["q"] * w).astype(BF16), (hd["k"] * w).astype(BF16), _NT,
                                  preferred_element_type=F32)
            hd["a"] = jnp.where(lv == level, a_l, hd["a"])

    for h, hd in enumerate(heads):
        q, k, b, vb = hd["q"], hd["k"], hd["b"], hd["vb"]
        st = st_ref[h]
        o = _dot(hd["a"].astype(BF16), vb) + lax.dot_general(
            (q * jnp.exp2(b)).astype(BF16), st.astype(BF16), _NT, preferred_element_type=F32)
        b_end = b[c - 1:c, :]
        k_end = (k * jnp.exp2(b_end - b)).astype(BF16)
        st_new = st * jnp.exp2(b_end) + lax.dot_general(vb, k_end, _TN, preferred_element_type=F32)
        st_ref[h] = st_new
        o_ref[0, :, h * HG_DIM:(h + 1) * HG_DIM] = (_rms(o, ng) * hd["og"]).astype(o_ref.dtype)

    @pl.when(ci == pl.num_programs(1) - 1)
    def _():
        for h in range(HG_HEADS):
            s_out_ref[0, h] = st_ref[h].T


def _hgrn_prompt(zh3, lb, ng, chunk):
    b, l, _ = zh3.shape
    return pl.pallas_call(
        functools.partial(_hgrn_prompt_kernel, chunk=chunk),
        grid=(b, l // chunk),
        in_specs=[pl.BlockSpec((1, chunk, 4 * HG_WIDTH), lambda i, j: (i, j, 0)),
                  pl.BlockSpec((1, HG_WIDTH), lambda i, j: (0, 0)),
                  pl.BlockSpec((1, HG_DIM), lambda i, j: (0, 0))],
        out_specs=[pl.BlockSpec((1, chunk, HG_WIDTH), lambda i, j: (i, j, 0)),
                   pl.BlockSpec((1, HG_HEADS, HG_DIM, HG_DIM), lambda i, j: (i, 0, 0, 0))],
        out_shape=[jax.ShapeDtypeStruct((b, l, HG_WIDTH), BF16),
                   jax.ShapeDtypeStruct((b, HG_HEADS, HG_DIM, HG_DIM), F32)],
        scratch_shapes=[pltpu.VMEM((HG_HEADS, HG_DIM, HG_DIM), F32), pltpu.VMEM((chunk, chunk), I32)],
        compiler_params=pltpu.CompilerParams(dimension_semantics=("arbitrary", "arbitrary"),
                                             vmem_limit_bytes=VMEM_LIMIT),
        name="hgrn_prompt",
    )(zh3, lb, ng)


def _hgrn_step_kernel(zh_ref, lb_ref, ng_ref, s_ref, o_ref, s_out_ref):
    zh = zh_ref[0]
    ng = ng_ref[...]
    eye = (lax.broadcasted_iota(I32, (HG_DIM, HG_DIM), 0) == lax.broadcasted_iota(I32, (HG_DIM, HG_DIM), 1))

    def column(row):
        return jnp.sum(jnp.where(eye, row, 0.0), axis=1, keepdims=True)

    for h in range(HG_HEADS):
        lb = lb_ref[:, h * HG_DIM:(h + 1) * HG_DIM]
        q, f, k, v, og = _hgrn_gates(zh, lb, h)
        s_new = column(f) * s_ref[0, h] + column(k) * v
        s_out_ref[0, h] = s_new
        o = jnp.sum(column(q) * s_new, axis=0, keepdims=True)
        o_ref[0, :, h * HG_DIM:(h + 1) * HG_DIM] = _rms(o, ng) * og


def _hgrn_step(zh3, lb, ng, state):
    b = zh3.shape[0]
    return pl.pallas_call(
        _hgrn_step_kernel,
        grid=(b,),
        in_specs=[pl.BlockSpec((1, 1, 4 * HG_WIDTH), lambda i: (i, 0, 0)),
                  pl.BlockSpec((1, HG_WIDTH), lambda i: (0, 0)),
                  pl.BlockSpec((1, HG_DIM), lambda i: (0, 0)),
                  pl.BlockSpec((1, HG_HEADS, HG_DIM, HG_DIM), lambda i: (i, 0, 0, 0))],
        out_specs=[pl.BlockSpec((1, 1, HG_WIDTH), lambda i: (i, 0, 0)),
                   pl.BlockSpec((1, HG_HEADS, HG_DIM, HG_DIM), lambda i: (i, 0, 0, 0))],
        out_shape=[jax.ShapeDtypeStruct((b, 1, HG_WIDTH), F32),
                   jax.ShapeDtypeStruct(state.shape, F32)],
        compiler_params=pltpu.CompilerParams(dimension_semantics=("arbitrary",)),
        name="hgrn_step",
    )(zh3, lb, ng, state)


def _topk_rank(gm, axis, n_candidates, idx):
    rank = jnp.zeros(gm.shape, F32)
    for m in range(n_candidates):
        other = lax.slice_in_dim(gm, m, m + 1, axis=axis)
        tie = jnp.where(other == gm, jnp.where(idx > m, 1.0, 0.0), 0.0)
        rank = rank + jnp.where(other > gm, 1.0, tie)
    return rank


MOBA_Q_TILE = 128
MOBA_HEADS_PER_STEP = 2
MOBA_LOOKAHEAD = 3


def _moba_prompt_kernel(qt_ref, kt_ref, vt_ref, ot_ref):
    l = qt_ref.shape[2]
    nb = l // MOBA_BLOCK
    tq = MOBA_Q_TILE
    scale = ATT_DIM ** -0.5
    blk_row = lax.broadcasted_iota(I32, (SUBLANES, l), 0)
    own_blk = lax.broadcasted_iota(I32, (SUBLANES, l), 1) // MOBA_BLOCK
    causal = {}
    for t0 in range(0, MOBA_BLOCK, tq):
        shape = (t0 + tq, tq)
        causal[t0] = lax.broadcasted_iota(I32, shape, 0) <= lax.broadcasted_iota(I32, shape, 1) + t0
    heads = []
    for hh in range(MOBA_HEADS_PER_STEP):
        chans = slice(hh * ATT_DIM, (hh + 1) * ATT_DIM)
        qt = qt_ref[0, chans, :]
        k_rows = kt_ref[0, chans, :].T
        kmean = jnp.sum(k_rows.reshape(nb, MOBA_BLOCK, ATT_DIM), axis=1) * (1.0 / MOBA_BLOCK)
        if nb < SUBLANES:
            kmean = jnp.concatenate([kmean, jnp.zeros((SUBLANES - nb, ATT_DIM), F32)], axis=0)
        gate = _dot_f32(kmean, qt, (((1,), (0,)), ((), ())))
        valid = blk_row < own_blk
        rank = _topk_rank(jnp.where(valid, gate, NEG_INF), 0, nb, blk_row)
        sel = jnp.where(valid, jnp.where(rank < MOBA_TOPK, 1.0, 0.0), 0.0)
        heads.append(dict(chans=chans, sel=sel, qtb=(qt * scale).astype(BF16), kb=k_rows.astype(BF16),
                          vtb=vt_ref[0, chans, :].astype(BF16)))

    tiles = [(hd, i, t0) for i in range(nb) for t0 in range(0, MOBA_BLOCK, tq) for hd in heads]

    def scores(tile):
        hd, i, t0 = tile
        q0 = i * MOBA_BLOCK + t0
        return _dot(hd["kb"][:q0 + tq], hd["qtb"][:, q0:q0 + tq])

    ahead = [scores(t) for t in tiles[:MOBA_LOOKAHEAD]]
    for n, (hd, i, t0) in enumerate(tiles):
        s = ahead.pop(0)
        if n + MOBA_LOOKAHEAD < len(tiles):
            ahead.append(scores(tiles[n + MOBA_LOOKAHEAD]))
        q0 = i * MOBA_BLOCK + t0
        pieces = [jnp.where(hd["sel"][j:j + 1, q0:q0 + tq] > 0.5, s[j * MOBA_BLOCK:(j + 1) * MOBA_BLOCK], NEG_INF)
                  for j in range(i)]
        pieces.append(jnp.where(causal[t0], s[i * MOBA_BLOCK:], NEG_INF))
        sm = pieces[0] if i == 0 else jnp.concatenate(pieces, axis=0)
        p = jnp.exp(sm - jnp.max(sm, axis=0, keepdims=True))
        den = jnp.sum(p, axis=0, keepdims=True)
        ot_ref[0, hd["chans"], q0:q0 + tq] = (_dot(hd["vtb"][:, :q0 + tq], p.astype(BF16)) / den).astype(ot_ref.dtype)


def _moba_prompt(qt3, kt3, vt3):
    b, _, l = qt3.shape
    assert l % MOBA_BLOCK == 0 and l // MOBA_BLOCK <= SUBLANES
    width = MOBA_HEADS_PER_STEP * ATT_DIM
    spec = pl.BlockSpec((1, width, l), lambda i, j: (i, j, 0))
    return pl.pallas_call(
        _moba_prompt_kernel,
        grid=(b, ATT_WIDTH // width),
        in_specs=[spec, spec, spec],
        out_specs=spec,
        out_shape=jax.ShapeDtypeStruct((b, ATT_WIDTH, l), BF16),
        compiler_params=pltpu.CompilerParams(dimension_semantics=("arbitrary", "arbitrary"),
                                             vmem_limit_bytes=VMEM_LIMIT),
        name="moba_prompt",
    )(qt3, kt3, vt3)


SEL_PAGES = 16


def _column(row, eye):
    return jnp.sum(jnp.where(eye, row, 0.0), axis=1, keepdims=True)


def _moba_scores_kernel(pt_ref, q_ref, kc_ref, s_ref, sel_ref, buf_ref, sem_ref, *, n_pages):
    bi = pl.program_id(0)
    ci = pl.program_id(1)
    nch = pl.num_programs(1)
    step = bi * nch + ci
    slot = step % 2
    nblk = n_pages // PAGES_PER_BLOCK

    def copies(b_idx, c_idx, slot_idx):
        return [pltpu.make_async_copy(kc_ref.at[0, pt_ref[b_idx, c_idx * SEL_PAGES + p]],
                                      buf_ref.at[slot_idx, p], sem_ref.at[slot_idx])
                for p in range(SEL_PAGES)]

    @pl.when(step == 0)
    def _():
        for cp in copies(bi, ci, slot):
            cp.start()

    @pl.when(step + 1 < pl.num_programs(0) * nch)
    def _():
        nxt = step + 1
        for cp in copies(nxt // nch, nxt % nch, 1 - slot):
            cp.start()

    for cp in copies(bi, ci, slot):
        cp.wait()

    eye = (lax.broadcasted_iota(I32, (ATT_DIM, ATT_DIM), 0) == lax.broadcasted_iota(I32, (ATT_DIM, ATT_DIM), 1))
    q = q_ref[0]
    q_cols = [_column(q[h:h + 1, :], eye) for h in range(ATT_HEADS)]
    for p in range(SEL_PAGES):
        rows = [jnp.sum(buf_ref[slot, p, h] * q_cols[h], axis=0, keepdims=True) for h in range(ATT_HEADS)]
        blk = (ci * SEL_PAGES + p) // PAGES_PER_BLOCK
        half = p % PAGES_PER_BLOCK
        s_ref[0, pl.ds(pl.multiple_of(blk * ATT_HEADS, ATT_HEADS), ATT_HEADS),
              half * PAGE_SIZE:(half + 1) * PAGE_SIZE] = jnp.concatenate(rows, axis=0)

    @pl.when(ci == nch - 1)
    def _():
        lane = lax.broadcasted_iota(I32, (ATT_HEADS, LANES), 1)
        gate = jnp.full((ATT_HEADS, LANES), NEG_INF, F32)
        for n in range(nblk):
            g_n = jnp.sum(s_ref[0, n * ATT_HEADS:(n + 1) * ATT_HEADS, :], axis=1, keepdims=True) * (1.0 / MOBA_BLOCK)
            gate = jnp.where(lane == n, g_n, gate)
        rank = _topk_rank(gate, 1, nblk, lane)
        lane_f = lane.astype(F32)
        for j in range(MOBA_TOPK):
            pick = jnp.sum(jnp.where(rank == float(j), jnp.where(lane < nblk, lane_f, 0.0), 0.0),
                           axis=1, keepdims=True)
            sel_ref[0, j] = pick.astype(I32)


def _moba_scores(page_table, q3, cache_kt):
    b, n_pages = page_table.shape
    nch = n_pages // SEL_PAGES
    nblk = n_pages // PAGES_PER_BLOCK
    grid_spec = pltpu.PrefetchScalarGridSpec(
        num_scalar_prefetch=1,
        grid=(b, nch),
        in_specs=[pl.BlockSpec((1, ATT_HEADS, ATT_DIM), lambda i, j, pt: (i, 0, 0)),
                  pl.BlockSpec(memory_space=pl.ANY)],
        out_specs=[pl.BlockSpec((1, nblk * ATT_HEADS, MOBA_BLOCK), lambda i, j, pt: (i, 0, 0)),
                   pl.BlockSpec((1, MOBA_TOPK, ATT_HEADS, 1), lambda i, j, pt: (i, 0, 0, 0))],
        scratch_shapes=[pltpu.VMEM((2, SEL_PAGES, ATT_HEADS, ATT_DIM, PAGE_SIZE), F32),
                        pltpu.SemaphoreType.DMA((2,))],
    )
    scores, sel = pl.pallas_call(
        functools.partial(_moba_scores_kernel, n_pages=n_pages),
        grid_spec=grid_spec,
        out_shape=[jax.ShapeDtypeStruct((b, nblk * ATT_HEADS, MOBA_BLOCK), F32),
                   jax.ShapeDtypeStruct((b, MOBA_TOPK, ATT_HEADS, 1), I32)],
        compiler_params=pltpu.CompilerParams(dimension_semantics=("arbitrary", "arbitrary"),
                                             vmem_limit_bytes=VMEM_LIMIT),
        name="moba_scores",
    )(page_table, q3, cache_kt)
    return scores, sel.reshape(b, MOBA_TOPK, ATT_HEADS)


ATT_PAGES = MOBA_TOPK * PAGES_PER_BLOCK


def _moba_attend_kernel(pt_ref, sel_ref, s_ref, q_ref, kn_ref, vn_ref, vc_ref, o_ref, vbuf, sem_ref):
    bi = pl.program_id(0)
    slot = bi % 2
    scale = ATT_DIM ** -0.5

    def copies(b_idx, slot_idx):
        out = []
        for h in range(ATT_HEADS):
            for j in range(MOBA_TOPK):
                blk = sel_ref[b_idx, j, h]
                for p in range(PAGES_PER_BLOCK):
                    page = pt_ref[b_idx, blk * PAGES_PER_BLOCK + p]
                    out.append(pltpu.make_async_copy(vc_ref.at[0, page, h],
                                                     vbuf.at[slot_idx, h, j * PAGES_PER_BLOCK + p],
                                                     sem_ref.at[slot_idx]))
        return out

    @pl.when(bi == 0)
    def _():
        for cp in copies(bi, slot):
            cp.start()

    @pl.when(bi + 1 < pl.num_programs(0))
    def _():
        for cp in copies(bi + 1, 1 - slot):
            cp.start()

    for cp in copies(bi, slot):
        cp.wait()

    eye = (lax.broadcasted_iota(I32, (ATT_DIM, LANES), 0) == lax.broadcasted_iota(I32, (ATT_DIM, LANES), 1))
    s_own_all = jnp.sum(q_ref[0] * kn_ref[0], axis=1, keepdims=True) * scale
    for h in range(ATT_HEADS):
        s_blk = [s_ref[0, pl.ds(sel_ref[bi, j, h] * ATT_HEADS + h, 1), :] * scale for j in range(MOBA_TOPK)]
        s_own = s_own_all[h:h + 1, :]
        mx = s_own
        for s_j in s_blk:
            mx = jnp.maximum(mx, jnp.max(s_j, axis=1, keepdims=True))
        p_blk = [jnp.exp(s_j - mx) for s_j in s_blk]
        p_own = jnp.exp(s_own - mx)
        den = p_own
        acc = jnp.zeros((ATT_DIM, PAGE_SIZE), F32)
        for j, p_j in enumerate(p_blk):
            den = den + jnp.sum(p_j, axis=1, keepdims=True)
            for p in range(PAGES_PER_BLOCK):
                acc = acc + p_j[:, p * PAGE_SIZE:(p + 1) * PAGE_SIZE] * vbuf[slot, h, j * PAGES_PER_BLOCK + p]
        o_col = jnp.sum(acc, axis=1, keepdims=True)
        o_row = jnp.sum(jnp.where(eye, o_col, 0.0), axis=0, keepdims=True)[:, :ATT_DIM]
        o_ref[0, h:h + 1, :] = (o_row + p_own * vn_ref[0, h:h + 1, :]) / den


def _moba_attend(page_table, sel, scores, q3, kn3, vn3, cache_vt):
    b = page_table.shape[0]
    row = pl.BlockSpec((1, ATT_HEADS, ATT_DIM), lambda i, pt, sl: (i, 0, 0))
    grid_spec = pltpu.PrefetchScalarGridSpec(
        num_scalar_prefetch=2,
        grid=(b,),
        in_specs=[pl.BlockSpec((1,) + scores.shape[1:], lambda i, pt, sl: (i, 0, 0)),
                  row, row, row, pl.BlockSpec(memory_space=pl.ANY)],
        out_specs=row,
        scratch_shapes=[pltpu.VMEM((2, ATT_HEADS, ATT_PAGES, ATT_DIM, PAGE_SIZE), F32),
                        pltpu.SemaphoreType.DMA((2,))],
    )
    return pl.pallas_call(
        _moba_attend_kernel,
        grid_spec=grid_spec,
        out_shape=jax.ShapeDtypeStruct((b, ATT_HEADS, ATT_DIM), F32),
        compiler_params=pltpu.CompilerParams(dimension_semantics=("arbitrary",), vmem_limit_bytes=VMEM_LIMIT),
        name="moba_attend",
    )(page_table, sel, scores, q3, kn3, vn3, cache_vt)


FFN_COLS = 256
FFN_LOOKAHEAD = 1


def _post_kernel(x_ref, ohg_ref, oatt_ref, prev_ref, wo_ref, gpm_ref, gpf_ref, gqf_ref, wi_ref, cw_ref, cb_ref,
                 wf_ref, y_ref, conv_ref, gate_ref, act_ref, *, decode):
    li = pl.program_id(1)
    tl = x_ref.shape[1]
    d_ff = wf_ref.shape[0]
    if decode:
        att = _dot(oatt_ref[0].astype(BF16), wo_ref[HG_WIDTH:, :])
    else:
        att = lax.dot_general(oatt_ref[0].astype(BF16), wo_ref[HG_WIDTH:, :], _TN, preferred_element_type=F32)
    mix = _dot(ohg_ref[0].astype(BF16), wo_ref[:HG_WIDTH, :]) + att
    x1 = x_ref[0] + _rms(mix, gpm_ref[...])
    h2 = _rms(x1, gpf_ref[...]).astype(BF16)

    if not decode:
        @pl.when(li == 0)
        def _():
            gate_ref[0:SUBLANES, :] = jnp.zeros((SUBLANES, d_ff), F32)
            gate_ref[SUBLANES - 2:SUBLANES - 1, :] = prev_ref[0, 0]
            gate_ref[SUBLANES - 1:SUBLANES, :] = prev_ref[0, 1]

    def project(c0):
        return (_dot(h2, wi_ref[:, c0:c0 + FFN_COLS]), _dot(h2, wi_ref[:, d_ff + c0:d_ff + c0 + FFN_COLS]))

    chunks = list(range(0, d_ff, FFN_COLS))
    ahead = [project(c0) for c0 in chunks[:FFN_LOOKAHEAD]]
    for n, c0 in enumerate(chunks):
        cols = slice(c0, c0 + FFN_COLS)
        gate, up = ahead.pop(0)
        if n + FFN_LOOKAHEAD < len(chunks):
            ahead.append(project(chunks[n + FFN_LOOKAHEAD]))
        w0 = cw_ref[0:1, cols]
        w1 = cw_ref[1:2, cols]
        w2 = cw_ref[2:3, cols]
        if decode:
            conv = prev_ref[0, 0, :, cols] * w0 + prev_ref[0, 1, :, cols] * w1 + gate * w2 + cb_ref[:, cols]
            conv_ref[0, 0, :, cols] = prev_ref[0, 1, :, cols]
            conv_ref[0, 1, :, cols] = gate
        else:
            gate_ref[SUBLANES:SUBLANES + tl, cols] = gate
            conv = (gate_ref[SUBLANES - 2:SUBLANES - 2 + tl, cols] * w0
                    + gate_ref[SUBLANES - 1:SUBLANES - 1 + tl, cols] * w1 + gate * w2 + cb_ref[:, cols])
        act_ref[:, cols] = (jax.nn.gelu(conv, approximate=True) * up).astype(BF16)
    y_ref[0] = x1 + _rms(_dot(act_ref[...], wf_ref[...]), gqf_ref[...])

    if not decode:
        last = gate_ref[tl:tl + SUBLANES, :]
        gate_ref[0:SUBLANES, :] = last

        @pl.when(li == pl.num_programs(1) - 1)
        def _():
            conv_ref[0, 0] = last[SUBLANES - 2:SUBLANES - 1, :]
            conv_ref[0, 1] = last[SUBLANES - 1:SUBLANES, :]


def _post(x3, ohg3, oatt3, prev, wo, gpm, gpf, gqf, wi, cw, cb, wf, tl, decode):
    b, l, d = x3.shape
    d_ff = wf.shape[0]
    row = lambda width: pl.BlockSpec((1, tl, width), lambda i, j: (i, j, 0))
    const = lambda shape: pl.BlockSpec(shape, lambda i, j: (0,) * len(shape), pipeline_mode=pl.Buffered(1))
    if decode:
        state = pl.BlockSpec((1, CONV_WIDTH - 1, tl, d_ff), lambda i, j: (i, 0, j, 0))
        att = row(ATT_WIDTH)
    else:
        state = pl.BlockSpec((1, CONV_WIDTH - 1, 1, d_ff), lambda i, j: (i, 0, 0, 0))
        att = pl.BlockSpec((1, ATT_WIDTH, tl), lambda i, j: (i, 0, j))
    return pl.pallas_call(
        functools.partial(_post_kernel, decode=decode),
        grid=(b, l // tl),
        in_specs=[row(d), row(HG_WIDTH), att, state,
                  const(wo.shape), const((1, d)), const((1, d)), const((1, d)),
                  const(wi.shape), const(cw.shape), const((1, d_ff)), const(wf.shape)],
        out_specs=[row(d), state],
        out_shape=[jax.ShapeDtypeStruct((b, l, d), F32),
                   jax.ShapeDtypeStruct(prev.shape, F32)],
        scratch_shapes=[pltpu.VMEM((tl + SUBLANES, d_ff), F32), pltpu.VMEM((tl, d_ff), BF16)],
        compiler_params=pltpu.CompilerParams(dimension_semantics=("arbitrary", "arbitrary"),
                                             vmem_limit_bytes=VMEM_LIMIT),
        name="post_decode" if decode else "post_prompt",
    )(x3, ohg3, oatt3, prev, wo, gpm, gpf, gqf, wi, cw, cb, wf)


INPROJ_ROWS = 512
POST_ROWS = 512
HGRN_CHUNK = 256


def kernel(x_prompt, x_sample, cache_k, cache_v, state_hgrn, state_conv, page_table, w_in, w_out, hg_lb_logits,
           hg_norm, g_pre_mix, g_post_mix, g_pre_ffn, g_post_ffn, w_ffn_in, conv_w, conv_b, w_ffn_out):
    depth = w_in.shape[0]
    assert depth == 1, "single-layer step"
    bp, seq, d = x_prompt.shape
    bs, t_dec, _ = x_sample.shape
    assert t_dec == 1
    n_pages = page_table.shape[1]
    past = n_pages * PAGE_SIZE
    lb_all = jnp.cumsum(jax.nn.softmax(hg_lb_logits.astype(F32), axis=0), axis=0)

    layer = 0
    lb = lb_all[layer][None, :]
    ng = hg_norm[layer][None, :]
    w_in_b = w_in[layer].astype(BF16)
    w_out_b = w_out[layer].astype(BF16)
    w_fi_b = w_ffn_in[layer].astype(BF16)
    w_fo_b = w_ffn_out[layer].astype(BF16)
    gains = [g[layer][None, :] for g in (g_pre_mix, g_post_mix, g_pre_ffn, g_post_ffn)]
    cw = conv_w[layer]
    cb = conv_b[layer][None, :]
    post_w = (w_out_b, gains[1], gains[2], gains[3], w_fi_b, cw, cb, w_fo_b)

    tm = min(INPROJ_ROWS, seq)
    cos_t, sin_t = _rope_tables_t(jnp.arange(seq, dtype=I32))
    w_h = w_in_b[:, :4 * HG_WIDTH]
    w_at = w_in_b[:, 4 * HG_WIDTH:].T
    zh, qt, kt, vt = _inproj_t(x_prompt, gains[0], w_h, w_at, cos_t, sin_t, tm)
    o_hg, s_p = _hgrn_prompt(zh.reshape(bp, seq, 4 * HG_WIDTH), lb, ng, min(HGRN_CHUNK, seq))
    o_att = _moba_prompt(qt, kt, vt)
    d_ff = w_ffn_out.shape[1]
    zero_prev = jnp.zeros((bp, CONV_WIDTH - 1, 1, d_ff), F32)
    y_p, conv_p = _post(x_prompt, o_hg, o_att, zero_prev, *post_w, tl=min(POST_ROWS, seq), decode=False)
    conv_p = conv_p.reshape(bp, CONV_WIDTH - 1, d_ff)

    tabs_s = _rope_tables(jnp.full((bs,), past, dtype=I32))
    zh_s, q_s, k_s, v_s = _inproj(x_sample.reshape(bs, d), gains[0], w_in_b, tabs_s, bs)
    o_hg_s, s_s = _hgrn_step(zh_s.reshape(bs, 1, 4 * HG_WIDTH), lb, ng, state_hgrn[layer])
    heads = lambda a: a.reshape(bs, ATT_HEADS, ATT_DIM)
    cache_kt = jnp.transpose(cache_k, (0, 1, 3, 4, 2))
    cache_vt = jnp.transpose(cache_v, (0, 1, 3, 4, 2))
    scores, sel = _moba_scores(page_table, heads(q_s), cache_kt)
    o_att_s = _moba_attend(page_table, sel, scores, heads(q_s), heads(k_s), heads(v_s), cache_vt)
    rows3 = lambda a: a.reshape(1, bs, a.shape[-1])
    prev_s = jnp.transpose(state_conv[layer], (1, 0, 2))[None]
    y_s, conv_s = _post(rows3(x_sample), rows3(o_hg_s), o_att_s.reshape(1, bs, ATT_WIDTH), prev_s, *post_w,
                        tl=bs, decode=True)
    y_s = y_s.reshape(bs, 1, d)
    conv_s = jnp.transpose(conv_s[0], (1, 0, 2))

    kv_p = lambda a: jnp.transpose(a.reshape(1, bp, ATT_HEADS, ATT_DIM, seq), (0, 1, 4, 2, 3))
    kv_s = lambda a: a.reshape(1, bs, 1, ATT_HEADS, ATT_DIM)
    return (y_p, y_s, kv_p(kt), kv_p(vt), s_p[None], conv_p[None],
            kv_s(k_s), kv_s(v_s), s_s[None], conv_s[None])
```

```python
import functools

import jax
import jax.numpy as jnp
from jax import lax
from jax.experimental import pallas as pl
from jax.experimental.pallas import tpu as pltpu

F32 = jnp.float32
BF16 = jnp.bfloat16
I32 = jnp.int32

HG_HEADS = 4
HG_DIM = 128
HG_WIDTH = HG_HEADS * HG_DIM
ATT_HEADS = 8
ATT_DIM = 64
ATT_WIDTH = ATT_HEADS * ATT_DIM
MOBA_BLOCK = 256
MOBA_TOPK = 3
ROPE_THETA = 500000.0
ROPE_DIMS = ATT_DIM // 4
ROPE_HALF = ROPE_DIMS // 2
CONV_WIDTH = 3
NORM_EPS = 1e-6
PAGE_SIZE = 128
PAGES_PER_BLOCK = MOBA_BLOCK // PAGE_SIZE

LANES = 128
SUBLANES = 8
VMEM_LIMIT = 56 * 1024 * 1024

NEG_INF = float("-inf")
LOG2_E = 1.4426950408889634
_NT = (((1,), (1,)), ((), ()))
_TN = (((0,), (0,)), ((), ()))


def _dot(a, b):
    return jnp.dot(a, b, preferred_element_type=F32)


def _rms(x, g):
    return x * lax.rsqrt(jnp.mean(x * x, axis=-1, keepdims=True) + NORM_EPS) * g


def _sigmoid(x):
    return 0.5 * jnp.tanh(0.5 * x) + 0.5


def _split3(x):
    x1 = x.astype(BF16)
    r1 = x - x1.astype(F32)
    x2 = r1.astype(BF16)
    r2 = r1 - x2.astype(F32)
    return x1, x2, r2.astype(BF16)


def _dot_f32(a, b, dims):
    a1, a2, a3 = _split3(a)
    b1, b2, b3 = _split3(b)
    dg = functools.partial(lax.dot_general, dimension_numbers=dims, preferred_element_type=F32)
    small = dg(a1, b3) + dg(a3, b1) + dg(a2, b2)
    mid = dg(a1, b2) + dg(a2, b1)
    return dg(a1, b1) + (mid + small)


def _inproj_kernel(x_ref, g_ref, w_ref, cos_ref, sa_ref, sb_ref, zh_ref, q_ref, k_ref, v_ref):
    h = _rms(x_ref[...], g_ref[...]).astype(BF16)
    hgw = 4 * HG_WIDTH
    for c in range(4):
        zh_ref[:, c * HG_WIDTH:(c + 1) * HG_WIDTH] = _dot(h, w_ref[:, c * HG_WIDTH:(c + 1) * HG_WIDTH])
    cos = cos_ref[...]
    sa = sa_ref[...]
    sb = sb_ref[...]
    for out_ref, off in ((q_ref, hgw), (k_ref, hgw + ATT_WIDTH)):
        a = _dot(h, w_ref[:, off:off + ATT_WIDTH])
        for j in range(ATT_WIDTH // LANES):
            aj = a[:, j * LANES:(j + 1) * LANES]
            out_ref[:, j * LANES:(j + 1) * LANES] = (
                aj * cos + pltpu.roll(aj, LANES - ROPE_HALF, 1) * sa + pltpu.roll(aj, ROPE_HALF, 1) * sb)
    v_ref[...] = _dot(h, w_ref[:, hgw + 2 * ATT_WIDTH:hgw + 3 * ATT_WIDTH])


def _rope_tables(pos):
    inv = jnp.power(jnp.float32(ROPE_THETA), -jnp.arange(ROPE_HALF, dtype=F32) * (2.0 / ROPE_DIMS))
    ang = pos.astype(F32)[:, None] * inv[None, :]
    cos = jnp.cos(ang)
    sin = jnp.sin(ang)
    n = pos.shape[0]
    rest = ATT_DIM - ROPE_DIMS
    cos_h = jnp.concatenate([cos, cos, jnp.ones((n, rest), F32)], axis=1)
    sa_h = jnp.concatenate([-sin, jnp.zeros((n, ATT_DIM - ROPE_HALF), F32)], axis=1)
    sb_h = jnp.concatenate([jnp.zeros((n, ROPE_HALF), F32), sin, jnp.zeros((n, rest), F32)], axis=1)
    rep = LANES // ATT_DIM
    return tuple(jnp.tile(t, (1, rep)) for t in (cos_h, sa_h, sb_h))


def _rope_tables_t(pos):
    inv = jnp.power(jnp.float32(ROPE_THETA), -jnp.arange(ROPE_HALF, dtype=F32) * (2.0 / ROPE_DIMS))
    ang = pos.astype(F32)[:, None] * inv[None, :]
    return jnp.cos(ang).T, jnp.sin(ang).T


def _inproj(x2d, g, w_bf16, tables, tm):
    rows, d = x2d.shape
    ncols = w_bf16.shape[1]
    npos = tables[0].shape[0] // tm
    tab_spec = pl.BlockSpec((tm, LANES), lambda i: (i % npos, 0))
    row_spec = lambda width: pl.BlockSpec((tm, width), lambda i: (i, 0))
    return pl.pallas_call(
        _inproj_kernel,
        grid=(rows // tm,),
        in_specs=[row_spec(d),
                  pl.BlockSpec((1, d), lambda i: (0, 0)),
                  pl.BlockSpec((d, ncols), lambda i: (0, 0), pipeline_mode=pl.Buffered(1)),
                  tab_spec, tab_spec, tab_spec],
        out_specs=[row_spec(4 * HG_WIDTH), row_spec(ATT_WIDTH), row_spec(ATT_WIDTH), row_spec(ATT_WIDTH)],
        out_shape=[jax.ShapeDtypeStruct((rows, 4 * HG_WIDTH), F32),
                   jax.ShapeDtypeStruct((rows, ATT_WIDTH), F32),
                   jax.ShapeDtypeStruct((rows, ATT_WIDTH), F32),
                   jax.ShapeDtypeStruct((rows, ATT_WIDTH), F32)],
        compiler_params=pltpu.CompilerParams(dimension_semantics=("arbitrary",), vmem_limit_bytes=VMEM_LIMIT),
        name="inproj",
    )(x2d, g, w_bf16, *tables)


def _att_project(idx, h, wat_ref, cos, sin, out_ref):
    a = lax.dot_general(wat_ref[idx * ATT_WIDTH:(idx + 1) * ATT_WIDTH, :], h, _NT,
                        preferred_element_type=F32)
    out_ref[0] = a
    if idx == 2:
        return
    for hd in range(ATT_HEADS):
        r = hd * ATT_DIM
        x1 = a[r:r + ROPE_HALF]
        x2 = a[r + ROPE_HALF:r + ROPE_DIMS]
        out_ref[0, r:r + ROPE_HALF, :] = x1 * cos - x2 * sin
        out_ref[0, r + ROPE_HALF:r + ROPE_DIMS, :] = x2 * cos + x1 * sin


def _hgrn_gates(zh_row, lb, h):
    sl = lambda part: slice(part * HG_WIDTH + h * HG_DIM, part * HG_WIDTH + (h + 1) * HG_DIM)
    hq = zh_row[:, sl(0)]
    hf = zh_row[:, sl(1)]
    v = zh_row[:, sl(2)]
    hg = zh_row[:, sl(3)]
    q = hq * _sigmoid(hq)
    f = lb + (1.0 - lb) * _sigmoid(hf)
    return q, f, 1.0 - f, v, hg * _sigmoid(hg)


def _level_reference(b, b3, sub, level, c):
    m = 1 << level
    if m >= SUBLANES:
        pieces = []
        for grp in range(c // (2 * m)):
            r = grp * 2 * m + m - 1
            pieces.append(jnp.broadcast_to(b[r:r + 1, :], (2 * m, HG_DIM)))
        return pieces[0] if len(pieces) == 1 else jnp.concatenate(pieces, axis=0)
    beta3 = None
    for g0 in range(0, SUBLANES, 2 * m):
        r = g0 + m - 1
        piece = jnp.broadcast_to(b3[:, r:r + 1, :], b3.shape)
        beta3 = piece if beta3 is None else jnp.where(sub >= g0, piece, beta3)
    return beta3.reshape(c, HG_DIM)


def _hgrn_level_table(c):
    rows = lax.broadcasted_iota(I32, (c, c), 0)
    cols = lax.broadcasted_iota(I32, (c, c), 1)
    x = rows ^ cols
    lev = jnp.zeros((c, c), I32)
    for j in range(1, c.bit_length() - 1):
        lev = lev + jnp.where(x >= (1 << j), 1, 0)
    return jnp.where(cols < rows, lev, jnp.where(cols == rows, -1, -2))


def _hgrn_chunk(zh, lb_ref, ng, st_ref, lv, o_ref, row0, between):
    c = zh.shape[0]
    nlev = c.bit_length() - 1
    tri = jnp.where(lax.broadcasted_iota(I32, (c, c), 1) <= lax.broadcasted_iota(I32, (c, c), 0), 1.0, 0.0).astype(BF16)
    sub = lax.broadcasted_iota(I32, (c // SUBLANES, SUBLANES, HG_DIM), 1)

    def tile(x, r, cc):
        return x[r * SUBLANES:(r + 1) * SUBLANES, cc * LANES:(cc + 1) * LANES]

    heads = []
    for h in range(HG_HEADS):
        lb = lb_ref[:, h * HG_DIM:(h + 1) * HG_DIM]
        q, f, k, v, og = _hgrn_gates(zh, lb, h)
        g1, g2, g3 = _split3(jnp.log(f))
        b = (_dot(tri, g1) + (_dot(tri, g2) + _dot(tri, g3))) * LOG2_E
        diag = jnp.sum(q * k, axis=-1, keepdims=True)
        tiles = [[jnp.zeros((SUBLANES, LANES), F32)] * (c // LANES) for _ in range(c // SUBLANES)]
        for r in range(c // SUBLANES):
            cc = (r * SUBLANES) // LANES
            tiles[r][cc] = jnp.where(tile(lv, r, cc) == -1, diag[r * SUBLANES:(r + 1) * SUBLANES], 0.0)
        heads.append(dict(q=q, k=k, og=og, b=b, b3=b.reshape(c // SUBLANES, SUBLANES, HG_DIM), vb=v.astype(BF16),
                          a=tiles))

    for level in range(nlev):
        if level in between:
            between[level]()
        m = 1 << level
        groups = range(c // SUBLANES)
        upper = [r for r in groups if m < SUBLANES or (r * SUBLANES) & m]
        for hd in heads:
            w = jnp.exp2(-jnp.abs(hd["b"] - _level_reference(hd["b"], hd["b3"], sub, level, c)))
            if m >= SUBLANES:
                rg = lambda x, r: x[r * SUBLANES:(r + 1) * SUBLANES]
                scaled = jnp.concatenate([rg(hd["q"] if r in upper else hd["k"], r) for r in groups], axis=0) * w
                lhs = jnp.concatenate([rg(scaled, r) for r in upper], axis=0).astype(BF16)
                a_l = lax.dot_general(lhs, scaled.astype(BF16), _NT, preferred_element_type=F32)
            else:
                a_l = lax.dot_general((hd["q"] * w).astype(BF16), (hd["k"] * w).astype(BF16), _NT,
                                      preferred_element_type=F32)
            for i, r in enumerate(upper):
                base = (r * SUBLANES) & ~(2 * m - 1)
                if m >= LANES:
                    for cc in range(base // LANES, (base + m) // LANES):
                        hd["a"][r][cc] = tile(a_l, i, cc)
                else:
                    cc = base // LANES
                    hd["a"][r][cc] = jnp.where(tile(lv, r, cc) == level, tile(a_l, i, cc), hd["a"][r][cc])

    for h, hd in enumerate(heads):
        q, k, b, vb = hd["q"], hd["k"], hd["b"], hd["vb"]
        st = st_ref[h]
        a = jnp.concatenate([row[0] if len(row) == 1 else jnp.concatenate(row, axis=1) for row in hd["a"]], axis=0)
        o = _dot(a.astype(BF16), vb) + lax.dot_general(
            (q * jnp.exp2(b)).astype(BF16), st.astype(BF16), _NT, preferred_element_type=F32)
        b_end = b[c - 1:c, :]
        k_end = (k * jnp.exp2(b_end - b)).astype(BF16)
        st_new = st * jnp.exp2(b_end) + lax.dot_general(vb, k_end, _TN, preferred_element_type=F32)
        st_ref[h] = st_new
        o_ref[0, row0:row0 + c, h * HG_DIM:(h + 1) * HG_DIM] = (_rms(o, ng) * hd["og"]).astype(o_ref.dtype)


def _prompt_mix_kernel(x_ref, g_ref, wh_ref, wat_ref, cos_ref, sin_ref, lb_ref, ng_ref,
                       qt_ref, kt_ref, vt_ref, o_ref, s_out_ref, st_ref, lv_ref, *, chunk):
    ti = pl.program_id(1)
    tm = x_ref.shape[1]

    @pl.when(ti == 0)
    def _():
        st_ref[...] = jnp.zeros_like(st_ref)
        lv_ref[...] = _hgrn_level_table(chunk)

    h = _rms(x_ref[0], g_ref[...]).astype(BF16)
    zh = jnp.concatenate([_dot(h, wh_ref[:, p * HG_WIDTH:(p + 1) * HG_WIDTH]) for p in range(4)], axis=1)
    cos = cos_ref[...]
    sin = sin_ref[...]
    lv = lv_ref[...]
    ng = ng_ref[...]
    att_out = [(0, qt_ref), (1, kt_ref), (2, vt_ref)]
    n_chunks = tm // chunk
    per_chunk = -(-len(att_out) // n_chunks)
    nlev = chunk.bit_length() - 1
    for ci in range(n_chunks):
        mine, att_out = att_out[:per_chunk], att_out[per_chunk:]
        between = {(n * nlev) // per_chunk: functools.partial(_att_project, idx, h, wat_ref, cos, sin, ref)
                   for n, (idx, ref) in enumerate(mine)}
        _hgrn_chunk(zh[ci * chunk:(ci + 1) * chunk], lb_ref, ng, st_ref, lv, o_ref, ci * chunk, between)

    @pl.when(ti == pl.num_programs(1) - 1)
    def _():
        for hd in range(HG_HEADS):
            s_out_ref[0, hd] = st_ref[hd].T


def _prompt_mix(x3, g, wh_bf16, wat_bf16, cos_t, sin_t, lb, ng, tm, chunk):
    b, l, d = x3.shape
    const = lambda shape: pl.BlockSpec(shape, lambda i, j: (0, 0), pipeline_mode=pl.Buffered(1))
    tab = pl.BlockSpec((ROPE_HALF, tm), lambda i, j: (0, j))
    att = pl.BlockSpec((1, ATT_WIDTH, tm), lambda i, j: (i, 0, j))
    att_shape = jax.ShapeDtypeStruct((b, ATT_WIDTH, l), F32)
    return pl.pallas_call(
        functools.partial(_prompt_mix_kernel, chunk=chunk),
        grid=(b, l // tm),
        in_specs=[pl.BlockSpec((1, tm, d), lambda i, j: (i, j, 0)), const((1, d)), const(wh_bf16.shape),
                  const(wat_bf16.shape), tab, tab, const((1, HG_WIDTH)), const((1, HG_DIM))],
        out_specs=[att, att, att,
                   pl.BlockSpec((1, tm, HG_WIDTH), lambda i, j: (i, j, 0)),
                   pl.BlockSpec((1, HG_HEADS, HG_DIM, HG_DIM), lambda i, j: (i, 0, 0, 0))],
        out_shape=[att_shape, att_shape, att_shape,
                   jax.ShapeDtypeStruct((b, l, HG_WIDTH), BF16),
                   jax.ShapeDtypeStruct((b, HG_HEADS, HG_DIM, HG_DIM), F32)],
        scratch_shapes=[pltpu.VMEM((HG_HEADS, HG_DIM, HG_DIM), F32), pltpu.VMEM((chunk, chunk), I32)],
        compiler_params=pltpu.CompilerParams(dimension_semantics=("arbitrary", "arbitrary"),
                                             vmem_limit_bytes=VMEM_LIMIT),
        name="prompt_mix",
    )(x3, g, wh_bf16, wat_bf16, cos_t, sin_t, lb, ng)


def _hgrn_step_kernel(zh_ref, lb_ref, ng_ref, s_ref, o_ref, s_out_ref):
    zh = zh_ref[0]
    ng = ng_ref[...]
    eye = (lax.broadcasted_iota(I32, (HG_DIM, HG_DIM), 0) == lax.broadcasted_iota(I32, (HG_DIM, HG_DIM), 1))

    def column(row):
        return jnp.sum(jnp.where(eye, row, 0.0), axis=1, keepdims=True)

    for h in range(HG_HEADS):
        lb = lb_ref[:, h * HG_DIM:(h + 1) * HG_DIM]
        q, f, k, v, og = _hgrn_gates(zh, lb, h)
        s_new = column(f) * s_ref[0, h] + column(k) * v
        s_out_ref[0, h] = s_new
        o = jnp.sum(column(q) * s_new, axis=0, keepdims=True)
        o_ref[0, :, h * HG_DIM:(h + 1) * HG_DIM] = _rms(o, ng) * og


def _hgrn_step(zh3, lb, ng, state):
    b = zh3.shape[0]
    return pl.pallas_call(
        _hgrn_step_kernel,
        grid=(b,),
        in_specs=[pl.BlockSpec((1, 1, 4 * HG_WIDTH), lambda i: (i, 0, 0)),
                  pl.BlockSpec((1, HG_WIDTH), lambda i: (0, 0)),
                  pl.BlockSpec((1, HG_DIM), lambda i: (0, 0)),
                  pl.BlockSpec((1, HG_HEADS, HG_DIM, HG_DIM), lambda i: (i, 0, 0, 0))],
        out_specs=[pl.BlockSpec((1, 1, HG_WIDTH), lambda i: (i, 0, 0)),
                   pl.BlockSpec((1, HG_HEADS, HG_DIM, HG_DIM), lambda i: (i, 0, 0, 0))],
        out_shape=[jax.ShapeDtypeStruct((b, 1, HG_WIDTH), F32),
                   jax.ShapeDtypeStruct(state.shape, F32)],
        compiler_params=pltpu.CompilerParams(dimension_semantics=("arbitrary",)),
        name="hgrn_step",
    )(zh3, lb, ng, state)


def _topk_rank(gm, axis, n_candidates, idx):
    rank = jnp.zeros(gm.shape, F32)
    for m in range(n_candidates):
        other = lax.slice_in_dim(gm, m, m + 1, axis=axis)
        tie = jnp.where(other == gm, jnp.where(idx > m, 1.0, 0.0), 0.0)
        rank = rank + jnp.where(other > gm, 1.0, tie)
    return rank


MOBA_Q_TILE = 128
MOBA_HEADS_PER_STEP = 2
MOBA_LOOKAHEAD = 3


def _moba_prompt_kernel(qt_ref, kt_ref, vt_ref, ot_ref):
    l = qt_ref.shape[2]
    nb = l // MOBA_BLOCK
    tq = MOBA_Q_TILE
    scale = ATT_DIM ** -0.5
    blk_row = lax.broadcasted_iota(I32, (SUBLANES, l), 0)
    own_blk = lax.broadcasted_iota(I32, (SUBLANES, l), 1) // MOBA_BLOCK
    causal = {}
    for t0 in range(0, MOBA_BLOCK, tq):
        shape = (t0 + tq, tq)
        causal[t0] = lax.broadcasted_iota(I32, shape, 0) <= lax.broadcasted_iota(I32, shape, 1) + t0
    heads = []
    for hh in range(MOBA_HEADS_PER_STEP):
        chans = slice(hh * ATT_DIM, (hh + 1) * ATT_DIM)
        qt = qt_ref[0, chans, :]
        k_rows = kt_ref[0, chans, :].T
        kmean = jnp.sum(k_rows.reshape(nb, MOBA_BLOCK, ATT_DIM), axis=1) * (1.0 / MOBA_BLOCK)
        if nb < SUBLANES:
            kmean = jnp.concatenate([kmean, jnp.zeros((SUBLANES - nb, ATT_DIM), F32)], axis=0)
        gate = _dot_f32(kmean, qt, (((1,), (0,)), ((), ())))
        valid = blk_row < own_blk
        rank = _topk_rank(jnp.where(valid, gate, NEG_INF), 0, nb, blk_row)
        sel = jnp.where(valid, jnp.where(rank < MOBA_TOPK, 1.0, 0.0), 0.0)
        heads.append(dict(chans=chans, sel=sel, qtb=(qt * scale).astype(BF16), kb=k_rows.astype(BF16),
                          vtb=vt_ref[0, chans, :].astype(BF16)))

    tiles = [(hd, i, t0) for i in range(nb) for t0 in range(0, MOBA_BLOCK, tq) for hd in heads]

    def scores(tile):
        hd, i, t0 = tile
        q0 = i * MOBA_BLOCK + t0
        return _dot(hd["kb"][:q0 + tq], hd["qtb"][:, q0:q0 + tq])

    ahead = [scores(t) for t in tiles[:MOBA_LOOKAHEAD]]
    for n, (hd, i, t0) in enumerate(tiles):
        s = ahead.pop(0)
        if n + MOBA_LOOKAHEAD < len(tiles):
            ahead.append(scores(tiles[n + MOBA_LOOKAHEAD]))
        q0 = i * MOBA_BLOCK + t0
        pieces = [jnp.where(hd["sel"][j:j + 1, q0:q0 + tq] > 0.5, s[j * MOBA_BLOCK:(j + 1) * MOBA_BLOCK], NEG_INF)
                  for j in range(i)]
        pieces.append(jnp.where(causal[t0], s[i * MOBA_BLOCK:], NEG_INF))
        sm = pieces[0] if i == 0 else jnp.concatenate(pieces, axis=0)
        p = jnp.exp(sm - jnp.max(sm, axis=0, keepdims=True))
        den = jnp.sum(p, axis=0, keepdims=True)
        ot_ref[0, hd["chans"], q0:q0 + tq] = (_dot(hd["vtb"][:, :q0 + tq], p.astype(BF16)) / den).astype(ot_ref.dtype)


def _moba_prompt(qt3, kt3, vt3):
    b, _, l = qt3.shape
    assert l % MOBA_BLOCK == 0 and l // MOBA_BLOCK <= SUBLANES
    width = MOBA_HEADS_PER_STEP * ATT_DIM
    spec = pl.BlockSpec((1, width, l), lambda i, j: (i, j, 0))
    return pl.pallas_call(
        _moba_prompt_kernel,
        grid=(b, ATT_WIDTH // width),
        in_specs=[spec, spec, spec],
        out_specs=spec,
        out_shape=jax.ShapeDtypeStruct((b, ATT_WIDTH, l), BF16),
        compiler_params=pltpu.CompilerParams(dimension_semantics=("arbitrary", "arbitrary"),
                                             vmem_limit_bytes=VMEM_LIMIT),
        name="moba_prompt",
    )(qt3, kt3, vt3)


SEL_PAGES = 16
SEL_SLOTS = 3


def _column(row, eye):
    return jnp.sum(jnp.where(eye, row, 0.0), axis=1, keepdims=True)


def _moba_scores_kernel(pt_ref, q_ref, kc_ref, s_ref, sel_ref, buf_ref, sem_ref, *, n_pages):
    bi = pl.program_id(0)
    ci = pl.program_id(1)
    nch = pl.num_programs(1)
    step = bi * nch + ci
    n_steps = pl.num_programs(0) * nch
    slot = step % SEL_SLOTS
    nblk = n_pages // PAGES_PER_BLOCK

    def copies(step_idx):
        b_idx = step_idx // nch
        c_idx = step_idx % nch
        slot_idx = step_idx % SEL_SLOTS
        return [pltpu.make_async_copy(kc_ref.at[0, pt_ref[b_idx, c_idx * SEL_PAGES + p]],
                                      buf_ref.at[slot_idx, p], sem_ref.at[slot_idx])
                for p in range(SEL_PAGES)]

    for ahead in range(SEL_SLOTS - 1):
        @pl.when(jnp.logical_and(step == 0, ahead < n_steps))
        def _():
            for cp in copies(step + ahead):
                cp.start()

    @pl.when(step + SEL_SLOTS - 1 < n_steps)
    def _():
        for cp in copies(step + SEL_SLOTS - 1):
            cp.start()

    for cp in copies(step):
        cp.wait()

    eye = (lax.broadcasted_iota(I32, (ATT_DIM, ATT_DIM), 0) == lax.broadcasted_iota(I32, (ATT_DIM, ATT_DIM), 1))
    q = q_ref[0]
    q_cols = [_column(q[h:h + 1, :], eye) for h in range(ATT_HEADS)]
    for p in range(SEL_PAGES):
        rows = [jnp.sum(buf_ref[slot, p, h] * q_cols[h], axis=0, keepdims=True) for h in range(ATT_HEADS)]
        blk = (ci * SEL_PAGES + p) // PAGES_PER_BLOCK
        half = p % PAGES_PER_BLOCK
        s_ref[0, pl.ds(pl.multiple_of(blk * ATT_HEADS, ATT_HEADS), ATT_HEADS),
              half * PAGE_SIZE:(half + 1) * PAGE_SIZE] = jnp.concatenate(rows, axis=0)

    @pl.when(ci == nch - 1)
    def _():
        lane = lax.broadcasted_iota(I32, (ATT_HEADS, LANES), 1)
        gate = jnp.full((ATT_HEADS, LANES), NEG_INF, F32)
        for n in range(nblk):
            g_n = jnp.sum(s_ref[0, n * ATT_HEADS:(n + 1) * ATT_HEADS, :], axis=1, keepdims=True) * (1.0 / MOBA_BLOCK)
            gate = jnp.where(lane == n, g_n, gate)
        rank = _topk_rank(gate, 1, nblk, lane)
        lane_f = lane.astype(F32)
        for j in range(MOBA_TOPK):
            pick = jnp.sum(jnp.where(rank == float(j), jnp.where(lane < nblk, lane_f, 0.0), 0.0),
                           axis=1, keepdims=True)
            sel_ref[0, j] = pick.astype(I32)


def _moba_scores(page_table, q3, cache_kt):
    b, n_pages = page_table.shape
    nch = n_pages // SEL_PAGES
    nblk = n_pages // PAGES_PER_BLOCK
    grid_spec = pltpu.PrefetchScalarGridSpec(
        num_scalar_prefetch=1,
        grid=(b, nch),
        in_specs=[pl.BlockSpec((1, ATT_HEADS, ATT_DIM), lambda i, j, pt: (i, 0, 0)),
                  pl.BlockSpec(memory_space=pl.ANY)],
        out_specs=[pl.BlockSpec((1, nblk * ATT_HEADS, MOBA_BLOCK), lambda i, j, pt: (i, 0, 0)),
                   pl.BlockSpec((1, MOBA_TOPK, ATT_HEADS, 1), lambda i, j, pt: (i, 0, 0, 0))],
        scratch_shapes=[pltpu.VMEM((SEL_SLOTS, SEL_PAGES, ATT_HEADS, ATT_DIM, PAGE_SIZE), F32),
                        pltpu.SemaphoreType.DMA((SEL_SLOTS,))],
    )
    scores, sel = pl.pallas_call(
        functools.partial(_moba_scores_kernel, n_pages=n_pages),
        grid_spec=grid_spec,
        out_shape=[jax.ShapeDtypeStruct((b, nblk * ATT_HEADS, MOBA_BLOCK), F32),
                   jax.ShapeDtypeStruct((b, MOBA_TOPK, ATT_HEADS, 1), I32)],
        compiler_params=pltpu.CompilerParams(dimension_semantics=("arbitrary", "arbitrary"),
                                             vmem_limit_bytes=VMEM_LIMIT),
        name="moba_scores",
    )(page_table, q3, cache_kt)
    return scores, sel.reshape(b, MOBA_TOPK, ATT_HEADS)


ATT_PAGES = MOBA_TOPK * PAGES_PER_BLOCK


def _moba_attend_kernel(pt_ref, sel_ref, s_ref, q_ref, kn_ref, vn_ref, vc_ref, o_ref, vbuf, sem_ref):
    bi = pl.program_id(0)
    slot = bi % 2
    scale = ATT_DIM ** -0.5

    def copies(b_idx, slot_idx):
        out = []
        for h in range(ATT_HEADS):
            for j in range(MOBA_TOPK):
                blk = sel_ref[b_idx, j, h]
                for p in range(PAGES_PER_BLOCK):
                    page = pt_ref[b_idx, blk * PAGES_PER_BLOCK + p]
                    out.append(pltpu.make_async_copy(vc_ref.at[0, page, h],
                                                     vbuf.at[slot_idx, h, j * PAGES_PER_BLOCK + p],
                                                     sem_ref.at[slot_idx]))
        return out

    @pl.when(bi == 0)
    def _():
        for cp in copies(bi, slot):
            cp.start()

    @pl.when(bi + 1 < pl.num_programs(0))
    def _():
        for cp in copies(bi + 1, 1 - slot):
            cp.start()

    for cp in copies(bi, slot):
        cp.wait()

    eye = (lax.broadcasted_iota(I32, (ATT_DIM, LANES), 0) == lax.broadcasted_iota(I32, (ATT_DIM, LANES), 1))
    s_own_all = jnp.sum(q_ref[0] * kn_ref[0], axis=1, keepdims=True) * scale
    for h in range(ATT_HEADS):
        s_blk = [s_ref[0, pl.ds(sel_ref[bi, j, h] * ATT_HEADS + h, 1), :] * scale for j in range(MOBA_TOPK)]
        s_own = s_own_all[h:h + 1, :]
        mx = s_own
        for s_j in s_blk:
            mx = jnp.maximum(mx, jnp.max(s_j, axis=1, keepdims=True))
        p_blk = [jnp.exp(s_j - mx) for s_j in s_blk]
        p_own = jnp.exp(s_own - mx)
        den = p_own
        acc = jnp.zeros((ATT_DIM, PAGE_SIZE), F32)
        for j, p_j in enumerate(p_blk):
            den = den + jnp.sum(p_j, axis=1, keepdims=True)
            for p in range(PAGES_PER_BLOCK):
                acc = acc + p_j[:, p * PAGE_SIZE:(p + 1) * PAGE_SIZE] * vbuf[slot, h, j * PAGES_PER_BLOCK + p]
        o_col = jnp.sum(acc, axis=1, keepdims=True)
        o_row = jnp.sum(jnp.where(eye, o_col, 0.0), axis=0, keepdims=True)[:, :ATT_DIM]
        o_ref[0, h:h + 1, :] = (o_row + p_own * vn_ref[0, h:h + 1, :]) / den


def _moba_attend(page_table, sel, scores, q3, kn3, vn3, cache_vt):
    b = page_table.shape[0]
    row = pl.BlockSpec((1, ATT_HEADS, ATT_DIM), lambda i, pt, sl: (i, 0, 0))
    grid_spec = pltpu.PrefetchScalarGridSpec(
        num_scalar_prefetch=2,
        grid=(b,),
        in_specs=[pl.BlockSpec((1,) + scores.shape[1:], lambda i, pt, sl: (i, 0, 0)),
                  row, row, row, pl.BlockSpec(memory_space=pl.ANY)],
        out_specs=row,
        scratch_shapes=[pltpu.VMEM((2, ATT_HEADS, ATT_PAGES, ATT_DIM, PAGE_SIZE), F32),
                        pltpu.SemaphoreType.DMA((2,))],
    )
    return pl.pallas_call(
        _moba_attend_kernel,
        grid_spec=grid_spec,
        out_shape=jax.ShapeDtypeStruct((b, ATT_HEADS, ATT_DIM), F32),
        compiler_params=pltpu.CompilerParams(dimension_semantics=("arbitrary",), vmem_limit_bytes=VMEM_LIMIT),
        name="moba_attend",
    )(page_table, sel, scores, q3, kn3, vn3, cache_vt)


FFN_COLS = 256
FFN_LOOKAHEAD = 1


def _post_kernel(x_ref, ohg_ref, oatt_ref, prev_ref, wo_ref, gpm_ref, gpf_ref, gqf_ref, wi_ref, cw_ref, cb_ref,
                 wf_ref, y_ref, conv_ref, gate_ref, act_ref, *, decode):
    li = pl.program_id(1)
    tl = x_ref.shape[1]
    d_ff = wf_ref.shape[0]
    if decode:
        att = _dot(oatt_ref[0].astype(BF16), wo_ref[HG_WIDTH:, :])
    else:
        att = lax.dot_general(oatt_ref[0].astype(BF16), wo_ref[HG_WIDTH:, :], _TN, preferred_element_type=F32)
    mix = _dot(ohg_ref[0].astype(BF16), wo_ref[:HG_WIDTH, :]) + att
    x1 = x_ref[0] + _rms(mix, gpm_ref[...])
    h2 = _rms(x1, gpf_ref[...]).astype(BF16)

    if not decode:
        @pl.when(li == 0)
        def _():
            gate_ref[0:SUBLANES, :] = jnp.zeros((SUBLANES, d_ff), F32)
            gate_ref[SUBLANES - 2:SUBLANES - 1, :] = prev_ref[0, 0]
            gate_ref[SUBLANES - 1:SUBLANES, :] = prev_ref[0, 1]

    def project(c0):
        return (_dot(h2, wi_ref[:, c0:c0 + FFN_COLS]), _dot(h2, wi_ref[:, d_ff + c0:d_ff + c0 + FFN_COLS]))

    chunks = list(range(0, d_ff, FFN_COLS))
    ahead = [project(c0) for c0 in chunks[:FFN_LOOKAHEAD]]
    for n, c0 in enumerate(chunks):
        cols = slice(c0, c0 + FFN_COLS)
        gate, up = ahead.pop(0)
        if n + FFN_LOOKAHEAD < len(chunks):
            ahead.append(project(chunks[n + FFN_LOOKAHEAD]))
        w0 = cw_ref[0:1, cols]
        w1 = cw_ref[1:2, cols]
        w2 = cw_ref[2:3, cols]
        if decode:
            conv = prev_ref[0, 0, :, cols] * w0 + prev_ref[0, 1, :, cols] * w1 + gate * w2 + cb_ref[:, cols]
            conv_ref[0, 0, :, cols] = prev_ref[0, 1, :, cols]
            conv_ref[0, 1, :, cols] = gate
        else:
            gate_ref[SUBLANES:SUBLANES + tl, cols] = gate
            conv = (gate_ref[SUBLANES - 2:SUBLANES - 2 + tl, cols] * w0
                    + gate_ref[SUBLANES - 1:SUBLANES - 1 + tl, cols] * w1 + gate * w2 + cb_ref[:, cols])
        act_ref[:, cols] = (jax.nn.gelu(conv, approximate=True) * up).astype(BF16)
    y_ref[0] = x1 + _rms(_dot(act_ref[...], wf_ref[...]), gqf_ref[...])

    if not decode:
        last = gate_ref[tl:tl + SUBLANES, :]
        gate_ref[0:SUBLANES, :] = last

        @pl.when(li == pl.num_programs(1) - 1)
        def _():
            conv_ref[0, 0] = last[SUBLANES - 2:SUBLANES - 1, :]
            conv_ref[0, 1] = last[SUBLANES - 1:SUBLANES, :]


def _post(x3, ohg3, oatt3, prev, wo, gpm, gpf, gqf, wi, cw, cb, wf, tl, decode):
    b, l, d = x3.shape
    d_ff = wf.shape[0]
    row = lambda width: pl.BlockSpec((1, tl, width), lambda i, j: (i, j, 0))
    const = lambda shape: pl.BlockSpec(shape, lambda i, j: (0,) * len(shape), pipeline_mode=pl.Buffered(1))
    if decode:
        state = pl.BlockSpec((1, CONV_WIDTH - 1, tl, d_ff), lambda i, j: (i, 0, j, 0))
        att = row(ATT_WIDTH)
    else:
        state = pl.BlockSpec((1, CONV_WIDTH - 1, 1, d_ff), lambda i, j: (i, 0, 0, 0))
        att = pl.BlockSpec((1, ATT_WIDTH, tl), lambda i, j: (i, 0, j))
    return pl.pallas_call(
        functools.partial(_post_kernel, decode=decode),
        grid=(b, l // tl),
        in_specs=[row(d), row(HG_WIDTH), att, state,
                  const(wo.shape), const((1, d)), const((1, d)), const((1, d)),
                  const(wi.shape), const(cw.shape), const((1, d_ff)), const(wf.shape)],
        out_specs=[row(d), state],
        out_shape=[jax.ShapeDtypeStruct((b, l, d), F32),
                   jax.ShapeDtypeStruct(prev.shape, F32)],
        scratch_shapes=[pltpu.VMEM((tl + SUBLANES, d_ff), F32), pltpu.VMEM((tl, d_ff), BF16)],
        compiler_params=pltpu.CompilerParams(dimension_semantics=("arbitrary", "arbitrary"),
                                             vmem_limit_bytes=VMEM_LIMIT),
        name="post_decode" if decode else "post_prompt",
    )(x3, ohg3, oatt3, prev, wo, gpm, gpf, gqf, wi, cw, cb, wf)


INPROJ_ROWS = 512
POST_ROWS = 512
HGRN_CHUNK = 256


def kernel(x_prompt, x_sample, cache_k, cache_v, state_hgrn, state_conv, page_table, w_in, w_out, hg_lb_logits,
           hg_norm, g_pre_mix, g_post_mix, g_pre_ffn, g_post_ffn, w_ffn_in, conv_w, conv_b, w_ffn_out):
    depth = w_in.shape[0]
    assert depth == 1, "single-layer step"
    bp, seq, d = x_prompt.shape
    bs, t_dec, _ = x_sample.shape
    assert t_dec == 1
    n_pages = page_table.shape[1]
    past = n_pages * PAGE_SIZE
    lb_all = jnp.cumsum(jax.nn.softmax(hg_lb_logits.astype(F32), axis=0), axis=0)

    layer = 0
    lb = lb_all[layer][None, :]
    ng = hg_norm[layer][None, :]
    w_in_b = w_in[layer].astype(BF16)
    w_out_b = w_out[layer].astype(BF16)
    w_fi_b = w_ffn_in[layer].astype(BF16)
    w_fo_b = w_ffn_out[layer].astype(BF16)
    gains = [g[layer][None, :] for g in (g_pre_mix, g_post_mix, g_pre_ffn, g_post_ffn)]
    cw = conv_w[layer]
    cb = conv_b[layer][None, :]
    post_w = (w_out_b, gains[1], gains[2], gains[3], w_fi_b, cw, cb, w_fo_b)

    tm = min(INPROJ_ROWS, seq)
    cos_t, sin_t = _rope_tables_t(jnp.arange(seq, dtype=I32))
    w_h = w_in_b[:, :4 * HG_WIDTH]
    w_at = w_in_b[:, 4 * HG_WIDTH:].T
    qt, kt, vt, o_hg, s_p = _prompt_mix(x_prompt, gains[0], w_h, w_at, cos_t, sin_t, lb, ng, tm,
                                        min(HGRN_CHUNK, seq))
    o_att = _moba_prompt(qt, kt, vt)
    d_ff = w_ffn_out.shape[1]
    zero_prev = jnp.zeros((bp, CONV_WIDTH - 1, 1, d_ff), F32)
    y_p, conv_p = _post(x_prompt, o_hg, o_att, zero_prev, *post_w, tl=min(POST_ROWS, seq), decode=False)
    conv_p = conv_p.reshape(bp, CONV_WIDTH - 1, d_ff)

    tabs_s = _rope_tables(jnp.full((bs,), past, dtype=I32))
    zh_s, q_s, k_s, v_s = _inproj(x_sample.reshape(bs, d), gains[0], w_in_b, tabs_s, bs)
    o_hg_s, s_s = _hgrn_step(zh_s.reshape(bs, 1, 4 * HG_WIDTH), lb, ng, state_hgrn[layer])
    heads = lambda a: a.reshape(bs, ATT_HEADS, ATT_DIM)
    cache_kt = jnp.transpose(cache_k, (0, 1, 3, 4, 2))
    cache_vt = jnp.transpose(cache_v, (0, 1, 3, 4, 2))
    scores, sel = _moba_scores(page_table, heads(q_s), cache_kt)
    o_att_s = _moba_attend(page_table, sel, scores, heads(q_s), heads(k_s), heads(v_s), cache_vt)
    rows3 = lambda a: a.reshape(1, bs, a.shape[-1])
    prev_s = jnp.transpose(state_conv[layer], (1, 0, 2))[None]
    y_s, conv_s = _post(rows3(x_sample), rows3(o_hg_s), o_att_s.reshape(1, bs, ATT_WIDTH), prev_s, *post_w,
                        tl=bs, decode=True)
    y_s = y_s.reshape(bs, 1, d)
    conv_s = jnp.transpose(conv_s[0], (1, 0, 2))

    kv_p = lambda a: jnp.transpose(a.reshape(1, bp, ATT_HEADS, ATT_DIM, seq), (0, 1, 4, 2, 3))
    kv_s = lambda a: a.reshape(1, bs, 1, ATT_HEADS, ATT_DIM)
    return (y_p, y_s, kv_p(kt), kv_p(vt), s_p[None], conv_p[None],
            kv_s(k_s), kv_s(v_s), s_s[None], conv_s[None])
```

```python
import functools

import jax
import jax.numpy as jnp
from jax import lax
from jax.experimental import pallas as pl
from jax.experimental.pallas import tpu as pltpu

F32 = jnp.float32
BF16 = jnp.bfloat16
I32 = jnp.int32

HG_HEADS = 4
HG_DIM = 128
HG_WIDTH = HG_HEADS * HG_DIM
ATT_HEADS = 8
ATT_DIM = 64
ATT_WIDTH = ATT_HEADS * ATT_DIM
MOBA_BLOCK = 256
MOBA_TOPK = 3
ROPE_THETA = 500000.0
ROPE_DIMS = ATT_DIM // 4
ROPE_HALF = ROPE_DIMS // 2
CONV_WIDTH = 3
NORM_EPS = 1e-6
PAGE_SIZE = 128
PAGES_PER_BLOCK = MOBA_BLOCK // PAGE_SIZE

LANES = 128
SUBLANES = 8
VMEM_LIMIT = 56 * 1024 * 1024

NEG_INF = float("-inf")
LOG2_E = 1.4426950408889634
_NT = (((1,), (1,)), ((), ()))
_TN = (((0,), (0,)), ((), ()))


def _dot(a, b):
    return jnp.dot(a, b, preferred_element_type=F32)


def _rms(x, g):
    return x * lax.rsqrt(jnp.mean(x * x, axis=-1, keepdims=True) + NORM_EPS) * g


def _sigmoid(x):
    return 0.5 * jnp.tanh(0.5 * x) + 0.5


def _column(row, eye):
    return jnp.sum(jnp.where(eye, row, 0.0), axis=1, keepdims=True)


def _split3(x):
    x1 = x.astype(BF16)
    r1 = x - x1.astype(F32)
    x2 = r1.astype(BF16)
    r2 = r1 - x2.astype(F32)
    return x1, x2, r2.astype(BF16)


def _dot_f32(a, b, dims):
    a1, a2, a3 = _split3(a)
    b1, b2, b3 = _split3(b)
    dg = functools.partial(lax.dot_general, dimension_numbers=dims, preferred_element_type=F32)
    small = dg(a1, b3) + dg(a3, b1) + dg(a2, b2)
    mid = dg(a1, b2) + dg(a2, b1)
    return dg(a1, b1) + (mid + small)


def _inproj_kernel(x_ref, g_ref, w_ref, cos_ref, sa_ref, sb_ref, zh_ref, q_ref, k_ref, v_ref):
    h = _rms(x_ref[...], g_ref[...]).astype(BF16)
    hgw = 4 * HG_WIDTH
    for c in range(4):
        zh_ref[:, c * HG_WIDTH:(c + 1) * HG_WIDTH] = _dot(h, w_ref[:, c * HG_WIDTH:(c + 1) * HG_WIDTH])
    cos = cos_ref[...]
    sa = sa_ref[...]
    sb = sb_ref[...]
    for out_ref, off in ((q_ref, hgw), (k_ref, hgw + ATT_WIDTH)):
        a = _dot(h, w_ref[:, off:off + ATT_WIDTH])
        for j in range(ATT_WIDTH // LANES):
            aj = a[:, j * LANES:(j + 1) * LANES]
            out_ref[:, j * LANES:(j + 1) * LANES] = (
                aj * cos + pltpu.roll(aj, LANES - ROPE_HALF, 1) * sa + pltpu.roll(aj, ROPE_HALF, 1) * sb)
    v_ref[...] = _dot(h, w_ref[:, hgw + 2 * ATT_WIDTH:hgw + 3 * ATT_WIDTH])


def _rope_tables(pos):
    inv = jnp.power(jnp.float32(ROPE_THETA), -jnp.arange(ROPE_HALF, dtype=F32) * (2.0 / ROPE_DIMS))
    ang = pos.astype(F32)[:, None] * inv[None, :]
    cos = jnp.cos(ang)
    sin = jnp.sin(ang)
    n = pos.shape[0]
    rest = ATT_DIM - ROPE_DIMS
    cos_h = jnp.concatenate([cos, cos, jnp.ones((n, rest), F32)], axis=1)
    sa_h = jnp.concatenate([-sin, jnp.zeros((n, ATT_DIM - ROPE_HALF), F32)], axis=1)
    sb_h = jnp.concatenate([jnp.zeros((n, ROPE_HALF), F32), sin, jnp.zeros((n, rest), F32)], axis=1)
    rep = LANES // ATT_DIM
    return tuple(jnp.tile(t, (1, rep)) for t in (cos_h, sa_h, sb_h))


def _rope_tables_t(pos):
    inv = jnp.power(jnp.float32(ROPE_THETA), -jnp.arange(ROPE_HALF, dtype=F32) * (2.0 / ROPE_DIMS))
    ang = pos.astype(F32)[:, None] * inv[None, :]
    return jnp.cos(ang).T, jnp.sin(ang).T


def _inproj(x2d, g, w_bf16, tables, tm):
    rows, d = x2d.shape
    ncols = w_bf16.shape[1]
    npos = tables[0].shape[0] // tm
    tab_spec = pl.BlockSpec((tm, LANES), lambda i: (i % npos, 0))
    row_spec = lambda width: pl.BlockSpec((tm, width), lambda i: (i, 0))
    return pl.pallas_call(
        _inproj_kernel,
        grid=(rows // tm,),
        in_specs=[row_spec(d),
                  pl.BlockSpec((1, d), lambda i: (0, 0)),
                  pl.BlockSpec((d, ncols), lambda i: (0, 0), pipeline_mode=pl.Buffered(1)),
                  tab_spec, tab_spec, tab_spec],
        out_specs=[row_spec(4 * HG_WIDTH), row_spec(ATT_WIDTH), row_spec(ATT_WIDTH), row_spec(ATT_WIDTH)],
        out_shape=[jax.ShapeDtypeStruct((rows, 4 * HG_WIDTH), F32),
                   jax.ShapeDtypeStruct((rows, ATT_WIDTH), F32),
                   jax.ShapeDtypeStruct((rows, ATT_WIDTH), F32),
                   jax.ShapeDtypeStruct((rows, ATT_WIDTH), F32)],
        compiler_params=pltpu.CompilerParams(dimension_semantics=("arbitrary",), vmem_limit_bytes=VMEM_LIMIT),
        name="inproj",
    )(x2d, g, w_bf16, *tables)


def _att_project(idx, h, wat_ref, cos, sin, out_ref):
    a = lax.dot_general(wat_ref[idx * ATT_WIDTH:(idx + 1) * ATT_WIDTH, :], h, _NT,
                        preferred_element_type=F32)
    out_ref[0] = a
    if idx == 2:
        return
    for hd in range(ATT_HEADS):
        r = hd * ATT_DIM
        x1 = a[r:r + ROPE_HALF]
        x2 = a[r + ROPE_HALF:r + ROPE_DIMS]
        out_ref[0, r:r + ROPE_HALF, :] = x1 * cos - x2 * sin
        out_ref[0, r + ROPE_HALF:r + ROPE_DIMS, :] = x2 * cos + x1 * sin


def _hgrn_gates(zh_row, lb, h):
    sl = lambda part: slice(part * HG_WIDTH + h * HG_DIM, part * HG_WIDTH + (h + 1) * HG_DIM)
    hq = zh_row[:, sl(0)]
    hf = zh_row[:, sl(1)]
    v = zh_row[:, sl(2)]
    hg = zh_row[:, sl(3)]
    q = hq * _sigmoid(hq)
    f = lb + (1.0 - lb) * _sigmoid(hf)
    return q, f, 1.0 - f, v, hg * _sigmoid(hg)


def _level_reference(b, b3, sub, level, c):
    m = 1 << level
    if m >= SUBLANES:
        pieces = []
        for grp in range(c // (2 * m)):
            r = grp * 2 * m + m - 1
            pieces.append(jnp.broadcast_to(b[r:r + 1, :], (2 * m, HG_DIM)))
        return pieces[0] if len(pieces) == 1 else jnp.concatenate(pieces, axis=0)
    beta3 = None
    for g0 in range(0, SUBLANES, 2 * m):
        r = g0 + m - 1
        piece = jnp.broadcast_to(b3[:, r:r + 1, :], b3.shape)
        beta3 = piece if beta3 is None else jnp.where(sub >= g0, piece, beta3)
    return beta3.reshape(c, HG_DIM)


def _hgrn_level_table(c):
    rows = lax.broadcasted_iota(I32, (c, c), 0)
    cols = lax.broadcasted_iota(I32, (c, c), 1)
    x = rows ^ cols
    lev = jnp.zeros((c, c), I32)
    for j in range(1, c.bit_length() - 1):
        lev = lev + jnp.where(x >= (1 << j), 1, 0)
    return jnp.where(cols < rows, lev, jnp.where(cols == rows, -1, -2))


def _hgrn_chunk(zh, lb_ref, ng, st_ref, lv, o_ref, row0, between):
    c = zh.shape[0]
    nlev = c.bit_length() - 1
    tri = jnp.where(lax.broadcasted_iota(I32, (c, c), 1) <= lax.broadcasted_iota(I32, (c, c), 0), 1.0, 0.0).astype(BF16)
    sub = lax.broadcasted_iota(I32, (c // SUBLANES, SUBLANES, HG_DIM), 1)

    def tile(x, r, cc):
        return x[r * SUBLANES:(r + 1) * SUBLANES, cc * LANES:(cc + 1) * LANES]

    heads = []
    for h in range(HG_HEADS):
        lb = lb_ref[:, h * HG_DIM:(h + 1) * HG_DIM]
        q, f, k, v, og = _hgrn_gates(zh, lb, h)
        g1, g2, g3 = _split3(jnp.log(f))
        b = (_dot(tri, g1) + (_dot(tri, g2) + _dot(tri, g3))) * LOG2_E
        diag = jnp.sum(q * k, axis=-1, keepdims=True)
        tiles = [[jnp.zeros((SUBLANES, LANES), F32)] * (c // LANES) for _ in range(c // SUBLANES)]
        for r in range(c // SUBLANES):
            cc = (r * SUBLANES) // LANES
            tiles[r][cc] = jnp.where(tile(lv, r, cc) == -1, diag[r * SUBLANES:(r + 1) * SUBLANES], 0.0)
        heads.append(dict(q=q, k=k, og=og, b=b, b3=b.reshape(c // SUBLANES, SUBLANES, HG_DIM), vb=v.astype(BF16),
                          a=tiles))

    for level in range(nlev):
        if level in between:
            between[level]()
        m = 1 << level
        groups = range(c // SUBLANES)
        upper = [r for r in groups if m < SUBLANES or (r * SUBLANES) & m]
        for hd in heads:
            w = jnp.exp2(-jnp.abs(hd["b"] - _level_reference(hd["b"], hd["b3"], sub, level, c)))
            if m >= SUBLANES:
                rg = lambda x, r: x[r * SUBLANES:(r + 1) * SUBLANES]
                scaled = jnp.concatenate([rg(hd["q"] if r in upper else hd["k"], r) for r in groups], axis=0) * w
                lhs = jnp.concatenate([rg(scaled, r) for r in upper], axis=0).astype(BF16)
                a_l = lax.dot_general(lhs, scaled.astype(BF16), _NT, preferred_element_type=F32)
            else:
                a_l = lax.dot_general((hd["q"] * w).astype(BF16), (hd["k"] * w).astype(BF16), _NT,
                                      preferred_element_type=F32)
            for i, r in enumerate(upper):
                base = (r * SUBLANES) & ~(2 * m - 1)
                if m >= LANES:
                    for cc in range(base // LANES, (base + m) // LANES):
                        hd["a"][r][cc] = tile(a_l, i, cc)
                else:
                    cc = base // LANES
                    hd["a"][r][cc] = jnp.where(tile(lv, r, cc) == level, tile(a_l, i, cc), hd["a"][r][cc])

    for h, hd in enumerate(heads):
        q, k, b, vb = hd["q"], hd["k"], hd["b"], hd["vb"]
        st = st_ref[h]
        a = jnp.concatenate([row[0] if len(row) == 1 else jnp.concatenate(row, axis=1) for row in hd["a"]], axis=0)
        o = _dot(a.astype(BF16), vb) + lax.dot_general(
            (q * jnp.exp2(b)).astype(BF16), st.astype(BF16), _NT, preferred_element_type=F32)
        b_end = b[c - 1:c, :]
        k_end = (k * jnp.exp2(b_end - b)).astype(BF16)
        st_new = st * jnp.exp2(b_end) + lax.dot_general(vb, k_end, _TN, preferred_element_type=F32)
        st_ref[h] = st_new
        o_ref[0, row0:row0 + c, h * HG_DIM:(h + 1) * HG_DIM] = (_rms(o, ng) * hd["og"]).astype(o_ref.dtype)


def _prompt_mix_kernel(x_ref, g_ref, wh_ref, wat_ref, cos_ref, sin_ref, lb_ref, ng_ref,
                       qt_ref, kt_ref, vt_ref, o_ref, s_out_ref, st_ref, lv_ref, *, chunk):
    ti = pl.program_id(1)
    tm = x_ref.shape[1]

    @pl.when(ti == 0)
    def _():
        st_ref[...] = jnp.zeros_like(st_ref)
        lv_ref[...] = _hgrn_level_table(chunk)

    h = _rms(x_ref[0], g_ref[...]).astype(BF16)
    zh = jnp.concatenate([_dot(h, wh_ref[:, p * HG_WIDTH:(p + 1) * HG_WIDTH]) for p in range(4)], axis=1)
    cos = cos_ref[...]
    sin = sin_ref[...]
    lv = lv_ref[...]
    ng = ng_ref[...]
    att_out = [(0, qt_ref), (1, kt_ref), (2, vt_ref)]
    n_chunks = tm // chunk
    per_chunk = -(-len(att_out) // n_chunks)
    nlev = chunk.bit_length() - 1
    for ci in range(n_chunks):
        mine, att_out = att_out[:per_chunk], att_out[per_chunk:]
        between = {(n * nlev) // per_chunk: functools.partial(_att_project, idx, h, wat_ref, cos, sin, ref)
                   for n, (idx, ref) in enumerate(mine)}
        _hgrn_chunk(zh[ci * chunk:(ci + 1) * chunk], lb_ref, ng, st_ref, lv, o_ref, ci * chunk, between)

    @pl.when(ti == pl.num_programs(1) - 1)
    def _():
        for hd in range(HG_HEADS):
            s_out_ref[0, hd] = st_ref[hd].T


def _prompt_mix(x3, g, wh_bf16, wat_bf16, cos_t, sin_t, lb, ng, tm, chunk):
    b, l, d = x3.shape
    const = lambda shape: pl.BlockSpec(shape, lambda i, j: (0, 0), pipeline_mode=pl.Buffered(1))
    tab = pl.BlockSpec((ROPE_HALF, tm), lambda i, j: (0, j))
    att = pl.BlockSpec((1, ATT_WIDTH, tm), lambda i, j: (i, 0, j))
    att_shape = jax.ShapeDtypeStruct((b, ATT_WIDTH, l), F32)
    return pl.pallas_call(
        functools.partial(_prompt_mix_kernel, chunk=chunk),
        grid=(b, l // tm),
        in_specs=[pl.BlockSpec((1, tm, d), lambda i, j: (i, j, 0)), const((1, d)), const(wh_bf16.shape),
                  const(wat_bf16.shape), tab, tab, const((1, HG_WIDTH)), const((1, HG_DIM))],
        out_specs=[att, att, att,
                   pl.BlockSpec((1, tm, HG_WIDTH), lambda i, j: (i, j, 0)),
                   pl.BlockSpec((1, HG_HEADS, HG_DIM, HG_DIM), lambda i, j: (i, 0, 0, 0))],
        out_shape=[att_shape, att_shape, att_shape,
                   jax.ShapeDtypeStruct((b, l, HG_WIDTH), BF16),
                   jax.ShapeDtypeStruct((b, HG_HEADS, HG_DIM, HG_DIM), F32)],
        scratch_shapes=[pltpu.VMEM((HG_HEADS, HG_DIM, HG_DIM), F32), pltpu.VMEM((chunk, chunk), I32)],
        compiler_params=pltpu.CompilerParams(dimension_semantics=("arbitrary", "arbitrary"),
                                             vmem_limit_bytes=VMEM_LIMIT),
        name="prompt_mix",
    )(x3, g, wh_bf16, wat_bf16, cos_t, sin_t, lb, ng)


def _hgrn_step_kernel(zh_ref, lb_ref, ng_ref, s_ref, o_ref, s_out_ref):
    zh = zh_ref[0]
    ng = ng_ref[...]
    eye = (lax.broadcasted_iota(I32, (HG_DIM, HG_DIM), 0) == lax.broadcasted_iota(I32, (HG_DIM, HG_DIM), 1))

    def column(row):
        return jnp.sum(jnp.where(eye, row, 0.0), axis=1, keepdims=True)

    for h in range(HG_HEADS):
        lb = lb_ref[:, h * HG_DIM:(h + 1) * HG_DIM]
        q, f, k, v, og = _hgrn_gates(zh, lb, h)
        s_new = column(f) * s_ref[0, h] + column(k) * v
        s_out_ref[0, h] = s_new
        o = jnp.sum(column(q) * s_new, axis=0, keepdims=True)
        o_ref[0, :, h * HG_DIM:(h + 1) * HG_DIM] = _rms(o, ng) * og


def _hgrn_step(zh3, lb, ng, state):
    b = zh3.shape[0]
    return pl.pallas_call(
        _hgrn_step_kernel,
        grid=(b,),
        in_specs=[pl.BlockSpec((1, 1, 4 * HG_WIDTH), lambda i: (i, 0, 0)),
                  pl.BlockSpec((1, HG_WIDTH), lambda i: (0, 0)),
                  pl.BlockSpec((1, HG_DIM), lambda i: (0, 0)),
                  pl.BlockSpec((1, HG_HEADS, HG_DIM, HG_DIM), lambda i: (i, 0, 0, 0))],
        out_specs=[pl.BlockSpec((1, 1, HG_WIDTH), lambda i: (i, 0, 0)),
                   pl.BlockSpec((1, HG_HEADS, HG_DIM, HG_DIM), lambda i: (i, 0, 0, 0))],
        out_shape=[jax.ShapeDtypeStruct((b, 1, HG_WIDTH), F32),
                   jax.ShapeDtypeStruct(state.shape, F32)],
        compiler_params=pltpu.CompilerParams(dimension_semantics=("arbitrary",)),
        name="hgrn_step",
    )(zh3, lb, ng, state)


def _topk_rank(gm, axis, n_candidates, idx):
    rank = jnp.zeros(gm.shape, F32)
    for m in range(n_candidates):
        other = lax.slice_in_dim(gm, m, m + 1, axis=axis)
        tie = jnp.where(other == gm, jnp.where(idx > m, 1.0, 0.0), 0.0)
        rank = rank + jnp.where(other > gm, 1.0, tie)
    return rank


MOBA_Q_TILE = 128
MOBA_HEADS_PER_STEP = 2
MOBA_LOOKAHEAD = 3


def _moba_prompt_kernel(pt_ref, qt_ref, kt_ref, vt_ref, qs_ref, kc_ref, ot_ref, s_ref, sel_ref, kbuf_ref, sem_ref,
                        *, n_pages):
    step = pl.program_id(0) * pl.num_programs(1) + pl.program_id(1)
    slot = step % 2

    def page_copies(step_idx):
        return [pltpu.make_async_copy(kc_ref.at[0, pt_ref[step_idx, p]], kbuf_ref.at[step_idx % 2, p],
                                      sem_ref.at[step_idx % 2]) for p in range(n_pages)]

    @pl.when(step == 0)
    def _():
        for cp in page_copies(step):
            cp.start()

    @pl.when(step + 1 < pl.num_programs(0) * pl.num_programs(1))
    def _():
        for cp in page_copies(step + 1):
            cp.start()

    for cp in page_copies(step):
        cp.wait()

    eye = (lax.broadcasted_iota(I32, (ATT_DIM, ATT_DIM), 0) == lax.broadcasted_iota(I32, (ATT_DIM, ATT_DIM), 1))
    q_dec = qs_ref[0]
    q_cols = [_column(q_dec[h:h + 1, :], eye) for h in range(ATT_HEADS)]

    def score_page(p):
        rows = [jnp.sum(kbuf_ref[slot, p, h] * q_cols[h], axis=0, keepdims=True) for h in range(ATT_HEADS)]
        blk, half = divmod(p, PAGES_PER_BLOCK)
        s_ref[0, blk * ATT_HEADS:(blk + 1) * ATT_HEADS, half * PAGE_SIZE:(half + 1) * PAGE_SIZE] = (
            jnp.concatenate(rows, axis=0))

    l = qt_ref.shape[2]
    nb = l // MOBA_BLOCK
    tq = MOBA_Q_TILE
    scale = ATT_DIM ** -0.5
    blk_row = lax.broadcasted_iota(I32, (SUBLANES, l), 0)
    own_blk = lax.broadcasted_iota(I32, (SUBLANES, l), 1) // MOBA_BLOCK
    causal = {}
    for t0 in range(0, MOBA_BLOCK, tq):
        shape = (t0 + tq, tq)
        causal[t0] = lax.broadcasted_iota(I32, shape, 0) <= lax.broadcasted_iota(I32, shape, 1) + t0
    heads = []
    for hh in range(MOBA_HEADS_PER_STEP):
        chans = slice(hh * ATT_DIM, (hh + 1) * ATT_DIM)
        qt = qt_ref[0, chans, :]
        k_rows = kt_ref[0, chans, :].T
        kmean = jnp.sum(k_rows.reshape(nb, MOBA_BLOCK, ATT_DIM), axis=1) * (1.0 / MOBA_BLOCK)
        if nb < SUBLANES:
            kmean = jnp.concatenate([kmean, jnp.zeros((SUBLANES - nb, ATT_DIM), F32)], axis=0)
        gate = _dot_f32(kmean, qt, (((1,), (0,)), ((), ())))
        valid = blk_row < own_blk
        rank = _topk_rank(jnp.where(valid, gate, NEG_INF), 0, nb, blk_row)
        sel = jnp.where(valid, jnp.where(rank < MOBA_TOPK, 1.0, 0.0), 0.0)
        heads.append(dict(chans=chans, sel=sel, qtb=(qt * scale).astype(BF16), kb=k_rows.astype(BF16),
                          vtb=vt_ref[0, chans, :].astype(BF16)))

    tiles = [(hd, i, t0) for i in range(nb) for t0 in range(0, MOBA_BLOCK, tq) for hd in heads]

    def scores(tile):
        hd, i, t0 = tile
        q0 = i * MOBA_BLOCK + t0
        return _dot(hd["kb"][:q0 + tq], hd["qtb"][:, q0:q0 + tq])

    ahead = [scores(t) for t in tiles[:MOBA_LOOKAHEAD]]
    pages_done = 0
    for n, (hd, i, t0) in enumerate(tiles):
        s = ahead.pop(0)
        if n + MOBA_LOOKAHEAD < len(tiles):
            ahead.append(scores(tiles[n + MOBA_LOOKAHEAD]))
        pages_due = ((n + 1) * n_pages) // len(tiles)
        for p in range(pages_done, pages_due):
            score_page(p)
        pages_done = pages_due
        q0 = i * MOBA_BLOCK + t0
        pieces = [jnp.where(hd["sel"][j:j + 1, q0:q0 + tq] > 0.5, s[j * MOBA_BLOCK:(j + 1) * MOBA_BLOCK], NEG_INF)
                  for j in range(i)]
        pieces.append(jnp.where(causal[t0], s[i * MOBA_BLOCK:], NEG_INF))
        sm = pieces[0] if i == 0 else jnp.concatenate(pieces, axis=0)
        p = jnp.exp(sm - jnp.max(sm, axis=0, keepdims=True))
        den = jnp.sum(p, axis=0, keepdims=True)
        ot_ref[0, hd["chans"], q0:q0 + tq] = (_dot(hd["vtb"][:, :q0 + tq], p.astype(BF16)) / den).astype(ot_ref.dtype)

    nblk = n_pages // PAGES_PER_BLOCK
    lane = lax.broadcasted_iota(I32, (ATT_HEADS, LANES), 1)
    gate = jnp.full((ATT_HEADS, LANES), NEG_INF, F32)
    for n in range(nblk):
        g_n = jnp.sum(s_ref[0, n * ATT_HEADS:(n + 1) * ATT_HEADS, :], axis=1, keepdims=True) * (1.0 / MOBA_BLOCK)
        gate = jnp.where(lane == n, g_n, gate)
    rank = _topk_rank(gate, 1, nblk, lane)
    lane_f = lane.astype(F32)
    for j in range(MOBA_TOPK):
        pick = jnp.sum(jnp.where(rank == float(j), jnp.where(lane < nblk, lane_f, 0.0), 0.0), axis=1, keepdims=True)
        sel_ref[0, j] = pick.astype(I32)


def _moba_prompt(page_table, qt3, kt3, vt3, q_dec, cache_kt):
    b, _, l = qt3.shape
    dec, n_pages = page_table.shape
    assert l % MOBA_BLOCK == 0 and l // MOBA_BLOCK <= SUBLANES
    width = MOBA_HEADS_PER_STEP * ATT_DIM
    nj = ATT_WIDTH // width
    assert dec == b * nj, "one decode sequence is scored per prompt attention step"
    nblk = n_pages // PAGES_PER_BLOCK
    assert nblk <= LANES
    spec = pl.BlockSpec((1, width, l), lambda i, j, pt: (i, j, 0))
    per_step = lambda shape: pl.BlockSpec((1,) + shape, lambda i, j, pt: (i * nj + j,) + (0,) * len(shape))
    grid_spec = pltpu.PrefetchScalarGridSpec(
        num_scalar_prefetch=1,
        grid=(b, nj),
        in_specs=[spec, spec, spec, per_step((ATT_HEADS, ATT_DIM)), pl.BlockSpec(memory_space=pl.ANY)],
        out_specs=[spec, per_step((nblk * ATT_HEADS, MOBA_BLOCK)), per_step((MOBA_TOPK, ATT_HEADS, 1))],
        scratch_shapes=[pltpu.VMEM((2, n_pages, ATT_HEADS, ATT_DIM, PAGE_SIZE), F32),
                        pltpu.SemaphoreType.DMA((2,))],
    )
    o_att, scores, sel = pl.pallas_call(
        functools.partial(_moba_prompt_kernel, n_pages=n_pages),
        grid_spec=grid_spec,
        out_shape=[jax.ShapeDtypeStruct((b, ATT_WIDTH, l), BF16),
                   jax.ShapeDtypeStruct((dec, nblk * ATT_HEADS, MOBA_BLOCK), F32),
                   jax.ShapeDtypeStruct((dec, MOBA_TOPK, ATT_HEADS, 1), I32)],
        compiler_params=pltpu.CompilerParams(dimension_semantics=("arbitrary", "arbitrary"),
                                             vmem_limit_bytes=VMEM_LIMIT),
        name="moba_prompt",
    )(page_table, qt3, kt3, vt3, q_dec, cache_kt)
    return o_att, scores, sel.reshape(dec, MOBA_TOPK, ATT_HEADS)


ATT_PAGES = MOBA_TOPK * PAGES_PER_BLOCK


def _moba_attend_kernel(pt_ref, sel_ref, s_ref, q_ref, kn_ref, vn_ref, vc_ref, o_ref, vbuf, sem_ref):
    bi = pl.program_id(0)
    slot = bi % 2
    scale = ATT_DIM ** -0.5

    def copies(b_idx, slot_idx):
        out = []
        for h in range(ATT_HEADS):
            for j in range(MOBA_TOPK):
                blk = sel_ref[b_idx, j, h]
                for p in range(PAGES_PER_BLOCK):
                    page = pt_ref[b_idx, blk * PAGES_PER_BLOCK + p]
                    out.append(pltpu.make_async_copy(vc_ref.at[0, page, h],
                                                     vbuf.at[slot_idx, h, j * PAGES_PER_BLOCK + p],
                                                     sem_ref.at[slot_idx]))
        return out

    @pl.when(bi == 0)
    def _():
        for cp in copies(bi, slot):
            cp.start()

    @pl.when(bi + 1 < pl.num_programs(0))
    def _():
        for cp in copies(bi + 1, 1 - slot):
            cp.start()

    for cp in copies(bi, slot):
        cp.wait()

    eye = (lax.broadcasted_iota(I32, (ATT_DIM, LANES), 0) == lax.broadcasted_iota(I32, (ATT_DIM, LANES), 1))
    s_own_all = jnp.sum(q_ref[0] * kn_ref[0], axis=1, keepdims=True) * scale
    for h in range(ATT_HEADS):
        s_blk = [s_ref[0, pl.ds(sel_ref[bi, j, h] * ATT_HEADS + h, 1), :] * scale for j in range(MOBA_TOPK)]
        s_own = s_own_all[h:h + 1, :]
        mx = s_own
        for s_j in s_blk:
            mx = jnp.maximum(mx, jnp.max(s_j, axis=1, keepdims=True))
        p_blk = [jnp.exp(s_j - mx) for s_j in s_blk]
        p_own = jnp.exp(s_own - mx)
        den = p_own
        acc = jnp.zeros((ATT_DIM, PAGE_SIZE), F32)
        for j, p_j in enumerate(p_blk):
            den = den + jnp.sum(p_j, axis=1, keepdims=True)
            for p in range(PAGES_PER_BLOCK):
                acc = acc + p_j[:, p * PAGE_SIZE:(p + 1) * PAGE_SIZE] * vbuf[slot, h, j * PAGES_PER_BLOCK + p]
        o_col = jnp.sum(acc, axis=1, keepdims=True)
        o_row = jnp.sum(jnp.where(eye, o_col, 0.0), axis=0, keepdims=True)[:, :ATT_DIM]
        o_ref[0, h:h + 1, :] = (o_row + p_own * vn_ref[0, h:h + 1, :]) / den


def _moba_attend(page_table, sel, scores, q3, kn3, vn3, cache_vt):
    b = page_table.shape[0]
    row = pl.BlockSpec((1, ATT_HEADS, ATT_DIM), lambda i, pt, sl: (i, 0, 0))
    grid_spec = pltpu.PrefetchScalarGridSpec(
        num_scalar_prefetch=2,
        grid=(b,),
        in_specs=[pl.BlockSpec((1,) + scores.shape[1:], lambda i, pt, sl: (i, 0, 0)),
                  row, row, row, pl.BlockSpec(memory_space=pl.ANY)],
        out_specs=row,
        scratch_shapes=[pltpu.VMEM((2, ATT_HEADS, ATT_PAGES, ATT_DIM, PAGE_SIZE), F32),
                        pltpu.SemaphoreType.DMA((2,))],
    )
    return pl.pallas_call(
        _moba_attend_kernel,
        grid_spec=grid_spec,
        out_shape=jax.ShapeDtypeStruct((b, ATT_HEADS, ATT_DIM), F32),
        compiler_params=pltpu.CompilerParams(dimension_semantics=("arbitrary",), vmem_limit_bytes=VMEM_LIMIT),
        name="moba_attend",
    )(page_table, sel, scores, q3, kn3, vn3, cache_vt)


FFN_COLS = 256
FFN_LOOKAHEAD = 1


def _post_kernel(x_ref, ohg_ref, oatt_ref, prev_ref, wo_ref, gpm_ref, gpf_ref, gqf_ref, wi_ref, cw_ref, cb_ref,
                 wf_ref, y_ref, conv_ref, gate_ref, act_ref, *, decode):
    li = pl.program_id(1)
    tl = x_ref.shape[1]
    d_ff = wf_ref.shape[0]
    if decode:
        att = _dot(oatt_ref[0].astype(BF16), wo_ref[HG_WIDTH:, :])
    else:
        att = lax.dot_general(oatt_ref[0].astype(BF16), wo_ref[HG_WIDTH:, :], _TN, preferred_element_type=F32)
    mix = _dot(ohg_ref[0].astype(BF16), wo_ref[:HG_WIDTH, :]) + att
    x1 = x_ref[0] + _rms(mix, gpm_ref[...])
    h2 = _rms(x1, gpf_ref[...]).astype(BF16)

    if not decode:
        @pl.when(li == 0)
        def _():
            gate_ref[0:SUBLANES, :] = jnp.zeros((SUBLANES, d_ff), F32)
            gate_ref[SUBLANES - 2:SUBLANES - 1, :] = prev_ref[0, 0]
            gate_ref[SUBLANES - 1:SUBLANES, :] = prev_ref[0, 1]

    def project(c0):
        return (_dot(h2, wi_ref[:, c0:c0 + FFN_COLS]), _dot(h2, wi_ref[:, d_ff + c0:d_ff + c0 + FFN_COLS]))

    chunks = list(range(0, d_ff, FFN_COLS))
    ahead = [project(c0) for c0 in chunks[:FFN_LOOKAHEAD]]
    for n, c0 in enumerate(chunks):
        cols = slice(c0, c0 + FFN_COLS)
        gate, up = ahead.pop(0)
        if n + FFN_LOOKAHEAD < len(chunks):
            ahead.append(project(chunks[n + FFN_LOOKAHEAD]))
        w0 = cw_ref[0:1, cols]
        w1 = cw_ref[1:2, cols]
        w2 = cw_ref[2:3, cols]
        if decode:
            conv = prev_ref[0, 0, :, cols] * w0 + prev_ref[0, 1, :, cols] * w1 + gate * w2 + cb_ref[:, cols]
            conv_ref[0, 0, :, cols] = prev_ref[0, 1, :, cols]
            conv_ref[0, 1, :, cols] = gate
        else:
            gate_ref[SUBLANES:SUBLANES + tl, cols] = gate
            conv = (gate_ref[SUBLANES - 2:SUBLANES - 2 + tl, cols] * w0
                    + gate_ref[SUBLANES - 1:SUBLANES - 1 + tl, cols] * w1 + gate * w2 + cb_ref[:, cols])
        act_ref[:, cols] = (jax.nn.gelu(conv, approximate=True) * up).astype(BF16)
    y_ref[0] = x1 + _rms(_dot(act_ref[...], wf_ref[...]), gqf_ref[...])

    if not decode:
        last = gate_ref[tl:tl + SUBLANES, :]
        gate_ref[0:SUBLANES, :] = last

        @pl.when(li == pl.num_programs(1) - 1)
        def _():
            conv_ref[0, 0] = last[SUBLANES - 2:SUBLANES - 1, :]
            conv_ref[0, 1] = last[SUBLANES - 1:SUBLANES, :]


def _post(x3, ohg3, oatt3, prev, wo, gpm, gpf, gqf, wi, cw, cb, wf, tl, decode):
    b, l, d = x3.shape
    d_ff = wf.shape[0]
    row = lambda width: pl.BlockSpec((1, tl, width), lambda i, j: (i, j, 0))
    const = lambda shape: pl.BlockSpec(shape, lambda i, j: (0,) * len(shape), pipeline_mode=pl.Buffered(1))
    if decode:
        state = pl.BlockSpec((1, CONV_WIDTH - 1, tl, d_ff), lambda i, j: (i, 0, j, 0))
        att = row(ATT_WIDTH)
    else:
        state = pl.BlockSpec((1, CONV_WIDTH - 1, 1, d_ff), lambda i, j: (i, 0, 0, 0))
        att = pl.BlockSpec((1, ATT_WIDTH, tl), lambda i, j: (i, 0, j))
    return pl.pallas_call(
        functools.partial(_post_kernel, decode=decode),
        grid=(b, l // tl),
        in_specs=[row(d), row(HG_WIDTH), att, state,
                  const(wo.shape), const((1, d)), const((1, d)), const((1, d)),
                  const(wi.shape), const(cw.shape), const((1, d_ff)), const(wf.shape)],
        out_specs=[row(d), state],
        out_shape=[jax.ShapeDtypeStruct((b, l, d), F32),
                   jax.ShapeDtypeStruct(prev.shape, F32)],
        scratch_shapes=[pltpu.VMEM((tl + SUBLANES, d_ff), F32), pltpu.VMEM((tl, d_ff), BF16)],
        compiler_params=pltpu.CompilerParams(dimension_semantics=("arbitrary", "arbitrary"),
                                             vmem_limit_bytes=VMEM_LIMIT),
        name="post_decode" if decode else "post_prompt",
    )(x3, ohg3, oatt3, prev, wo, gpm, gpf, gqf, wi, cw, cb, wf)


INPROJ_ROWS = 512
POST_ROWS = 512
HGRN_CHUNK = 256


def kernel(x_prompt, x_sample, cache_k, cache_v, state_hgrn, state_conv, page_table, w_in, w_out, hg_lb_logits,
           hg_norm, g_pre_mix, g_post_mix, g_pre_ffn, g_post_ffn, w_ffn_in, conv_w, conv_b, w_ffn_out):
    depth = w_in.shape[0]
    assert depth == 1, "single-layer step"
    bp, seq, d = x_prompt.shape
    bs, t_dec, _ = x_sample.shape
    assert t_dec == 1
    n_pages = page_table.shape[1]
    past = n_pages * PAGE_SIZE
    lb_all = jnp.cumsum(jax.nn.softmax(hg_lb_logits.astype(F32), axis=0), axis=0)

    layer = 0
    lb = lb_all[layer][None, :]
    ng = hg_norm[layer][None, :]
    w_in_b = w_in[layer].astype(BF16)
    w_out_b = w_out[layer].astype(BF16)
    w_fi_b = w_ffn_in[layer].astype(BF16)
    w_fo_b = w_ffn_out[layer].astype(BF16)
    gains = [g[layer][None, :] for g in (g_pre_mix, g_post_mix, g_pre_ffn, g_post_ffn)]
    cw = conv_w[layer]
    cb = conv_b[layer][None, :]
    post_w = (w_out_b, gains[1], gains[2], gains[3], w_fi_b, cw, cb, w_fo_b)

    tabs_s = _rope_tables(jnp.full((bs,), past, dtype=I32))
    zh_s, q_s, k_s, v_s = _inproj(x_sample.reshape(bs, d), gains[0], w_in_b, tabs_s, bs)
    heads = lambda a: a.reshape(bs, ATT_HEADS, ATT_DIM)
    cache_kt = jnp.transpose(cache_k, (0, 1, 3, 4, 2))
    cache_vt = jnp.transpose(cache_v, (0, 1, 3, 4, 2))

    tm = min(INPROJ_ROWS, seq)
    cos_t, sin_t = _rope_tables_t(jnp.arange(seq, dtype=I32))
    w_h = w_in_b[:, :4 * HG_WIDTH]
    w_at = w_in_b[:, 4 * HG_WIDTH:].T
    qt, kt, vt, o_hg, s_p = _prompt_mix(x_prompt, gains[0], w_h, w_at, cos_t, sin_t, lb, ng, tm,
                                        min(HGRN_CHUNK, seq))
    o_att, scores, sel = _moba_prompt(page_table, qt, kt, vt, heads(q_s), cache_kt)
    d_ff = w_ffn_out.shape[1]
    zero_prev = jnp.zeros((bp, CONV_WIDTH - 1, 1, d_ff), F32)
    y_p, conv_p = _post(x_prompt, o_hg, o_att, zero_prev, *post_w, tl=min(POST_ROWS, seq), decode=False)
    conv_p = conv_p.reshape(bp, CONV_WIDTH - 1, d_ff)

    o_hg_s, s_s = _hgrn_step(zh_s.reshape(bs, 1, 4 * HG_WIDTH), lb, ng, state_hgrn[layer])
    o_att_s = _moba_attend(page_table, sel, scores, heads(q_s), heads(k_s), heads(v_s), cache_vt)
    rows3 = lambda a: a.reshape(1, bs, a.shape[-1])
    prev_s = jnp.transpose(state_conv[layer], (1, 0, 2))[None]
    y_s, conv_s = _post(rows3(x_sample), rows3(o_hg_s), o_att_s.reshape(1, bs, ATT_WIDTH), prev_s, *post_w,
                        tl=bs, decode=True)
    y_s = y_s.reshape(bs, 1, d)
    conv_s = jnp.transpose(conv_s[0], (1, 0, 2))

    kv_p = lambda a: jnp.transpose(a.reshape(1, bp, ATT_HEADS, ATT_DIM, seq), (0, 1, 4, 2, 3))
    kv_s = lambda a: a.reshape(1, bs, 1, ATT_HEADS, ATT_DIM)
    return (y_p, y_s, kv_p(kt), kv_p(vt), s_p[None], conv_p[None],
            kv_s(k_s), kv_s(v_s), s_s[None], conv_s[None])
```

```python
import functools

import jax
import jax.numpy as jnp
from jax import lax
from jax.experimental import pallas as pl
from jax.experimental.pallas import tpu as pltpu

F32 = jnp.float32
BF16 = jnp.bfloat16
I32 = jnp.int32

HG_HEADS = 4
HG_DIM = 128
HG_WIDTH = HG_HEADS * HG_DIM
ATT_HEADS = 8
ATT_DIM = 64
ATT_WIDTH = ATT_HEADS * ATT_DIM
MOBA_BLOCK = 256
MOBA_TOPK = 3
ROPE_THETA = 500000.0
ROPE_DIMS = ATT_DIM // 4
ROPE_HALF = ROPE_DIMS // 2
CONV_WIDTH = 3
NORM_EPS = 1e-6
PAGE_SIZE = 128
PAGES_PER_BLOCK = MOBA_BLOCK // PAGE_SIZE

LANES = 128
SUBLANES = 8
VMEM_LIMIT = 56 * 1024 * 1024

NEG_INF = float("-inf")
LOG2_E = 1.4426950408889634
_NT = (((1,), (1,)), ((), ()))
_TN = (((0,), (0,)), ((), ()))


def _dot(a, b):
    return jnp.dot(a, b, preferred_element_type=F32)


def _rms(x, g):
    return x * lax.rsqrt(jnp.mean(x * x, axis=-1, keepdims=True) + NORM_EPS) * g


def _sigmoid(x):
    return 0.5 * jnp.tanh(0.5 * x) + 0.5


def _column(row, eye):
    return jnp.sum(jnp.where(eye, row, 0.0), axis=1, keepdims=True)


def _split3(x):
    x1 = x.astype(BF16)
    r1 = x - x1.astype(F32)
    x2 = r1.astype(BF16)
    r2 = r1 - x2.astype(F32)
    return x1, x2, r2.astype(BF16)


def _dot_f32(a, b, dims):
    a1, a2, a3 = _split3(a)
    b1, b2, b3 = _split3(b)
    dg = functools.partial(lax.dot_general, dimension_numbers=dims, preferred_element_type=F32)
    small = dg(a1, b3) + dg(a3, b1) + dg(a2, b2)
    mid = dg(a1, b2) + dg(a2, b1)
    return dg(a1, b1) + (mid + small)


def _inproj_kernel(x_ref, g_ref, w_ref, cos_ref, sa_ref, sb_ref, zh_ref, q_ref, k_ref, v_ref):
    h = _rms(x_ref[...], g_ref[...]).astype(BF16)
    hgw = 4 * HG_WIDTH
    for c in range(4):
        zh_ref[:, c * HG_WIDTH:(c + 1) * HG_WIDTH] = _dot(h, w_ref[:, c * HG_WIDTH:(c + 1) * HG_WIDTH])
    cos = cos_ref[...]
    sa = sa_ref[...]
    sb = sb_ref[...]
    for out_ref, off in ((q_ref, hgw), (k_ref, hgw + ATT_WIDTH)):
        a = _dot(h, w_ref[:, off:off + ATT_WIDTH])
        for j in range(ATT_WIDTH // LANES):
            aj = a[:, j * LANES:(j + 1) * LANES]
            out_ref[:, j * LANES:(j + 1) * LANES] = (
                aj * cos + pltpu.roll(aj, LANES - ROPE_HALF, 1) * sa + pltpu.roll(aj, ROPE_HALF, 1) * sb)
    v_ref[...] = _dot(h, w_ref[:, hgw + 2 * ATT_WIDTH:hgw + 3 * ATT_WIDTH])


def _rope_tables(pos):
    inv = jnp.power(jnp.float32(ROPE_THETA), -jnp.arange(ROPE_HALF, dtype=F32) * (2.0 / ROPE_DIMS))
    ang = pos.astype(F32)[:, None] * inv[None, :]
    cos = jnp.cos(ang)
    sin = jnp.sin(ang)
    n = pos.shape[0]
    rest = ATT_DIM - ROPE_DIMS
    cos_h = jnp.concatenate([cos, cos, jnp.ones((n, rest), F32)], axis=1)
    sa_h = jnp.concatenate([-sin, jnp.zeros((n, ATT_DIM - ROPE_HALF), F32)], axis=1)
    sb_h = jnp.concatenate([jnp.zeros((n, ROPE_HALF), F32), sin, jnp.zeros((n, rest), F32)], axis=1)
    rep = LANES // ATT_DIM
    return tuple(jnp.tile(t, (1, rep)) for t in (cos_h, sa_h, sb_h))


def _rope_tables_t(pos):
    inv = jnp.power(jnp.float32(ROPE_THETA), -jnp.arange(ROPE_HALF, dtype=F32) * (2.0 / ROPE_DIMS))
    ang = pos.astype(F32)[:, None] * inv[None, :]
    return jnp.cos(ang).T, jnp.sin(ang).T


def _inproj(x2d, g, w_bf16, tables, tm):
    rows, d = x2d.shape
    ncols = w_bf16.shape[1]
    npos = tables[0].shape[0] // tm
    tab_spec = pl.BlockSpec((tm, LANES), lambda i: (i % npos, 0))
    row_spec = lambda width: pl.BlockSpec((tm, width), lambda i: (i, 0))
    return pl.pallas_call(
        _inproj_kernel,
        grid=(rows // tm,),
        in_specs=[row_spec(d),
                  pl.BlockSpec((1, d), lambda i: (0, 0)),
                  pl.BlockSpec((d, ncols), lambda i: (0, 0), pipeline_mode=pl.Buffered(1)),
                  tab_spec, tab_spec, tab_spec],
        out_specs=[row_spec(4 * HG_WIDTH), row_spec(ATT_WIDTH), row_spec(ATT_WIDTH), row_spec(ATT_WIDTH)],
        out_shape=[jax.ShapeDtypeStruct((rows, 4 * HG_WIDTH), F32),
                   jax.ShapeDtypeStruct((rows, ATT_WIDTH), F32),
                   jax.ShapeDtypeStruct((rows, ATT_WIDTH), F32),
                   jax.ShapeDtypeStruct((rows, ATT_WIDTH), F32)],
        compiler_params=pltpu.CompilerParams(dimension_semantics=("arbitrary",), vmem_limit_bytes=VMEM_LIMIT),
        name="inproj",
    )(x2d, g, w_bf16, *tables)


def _att_project(idx, h, wat_ref, cos, sin, out_ref):
    a = lax.dot_general(wat_ref[idx * ATT_WIDTH:(idx + 1) * ATT_WIDTH, :], h, _NT,
                        preferred_element_type=F32)
    out_ref[0] = a
    if idx == 2:
        return
    for hd in range(ATT_HEADS):
        r = hd * ATT_DIM
        x1 = a[r:r + ROPE_HALF]
        x2 = a[r + ROPE_HALF:r + ROPE_DIMS]
        out_ref[0, r:r + ROPE_HALF, :] = x1 * cos - x2 * sin
        out_ref[0, r + ROPE_HALF:r + ROPE_DIMS, :] = x2 * cos + x1 * sin


def _hgrn_gates(zh_row, lb, h):
    sl = lambda part: slice(part * HG_WIDTH + h * HG_DIM, part * HG_WIDTH + (h + 1) * HG_DIM)
    hq = zh_row[:, sl(0)]
    hf = zh_row[:, sl(1)]
    v = zh_row[:, sl(2)]
    hg = zh_row[:, sl(3)]
    q = hq * _sigmoid(hq)
    f = lb + (1.0 - lb) * _sigmoid(hf)
    return q, f, 1.0 - f, v, hg * _sigmoid(hg)


def _level_reference(b, b3, sub, level, c):
    m = 1 << level
    if m >= SUBLANES:
        pieces = []
        for grp in range(c // (2 * m)):
            r = grp * 2 * m + m - 1
            pieces.append(jnp.broadcast_to(b[r:r + 1, :], (2 * m, HG_DIM)))
        return pieces[0] if len(pieces) == 1 else jnp.concatenate(pieces, axis=0)
    beta3 = None
    for g0 in range(0, SUBLANES, 2 * m):
        r = g0 + m - 1
        piece = jnp.broadcast_to(b3[:, r:r + 1, :], b3.shape)
        beta3 = piece if beta3 is None else jnp.where(sub >= g0, piece, beta3)
    return beta3.reshape(c, HG_DIM)


def _hgrn_level_table(c):
    rows = lax.broadcasted_iota(I32, (c, c), 0)
    cols = lax.broadcasted_iota(I32, (c, c), 1)
    x = rows ^ cols
    lev = jnp.zeros((c, c), I32)
    for j in range(1, c.bit_length() - 1):
        lev = lev + jnp.where(x >= (1 << j), 1, 0)
    return jnp.where(cols < rows, lev, jnp.where(cols == rows, -1, -2))


def _hgrn_chunk(zh, lb_ref, ng, st_ref, lv, o_ref, row0, between):
    c = zh.shape[0]
    nlev = c.bit_length() - 1
    tri = jnp.where(lax.broadcasted_iota(I32, (c, c), 1) <= lax.broadcasted_iota(I32, (c, c), 0), 1.0, 0.0).astype(BF16)
    sub = lax.broadcasted_iota(I32, (c // SUBLANES, SUBLANES, HG_DIM), 1)

    def tile(x, r, cc):
        return x[r * SUBLANES:(r + 1) * SUBLANES, cc * LANES:(cc + 1) * LANES]

    heads = []
    for h in range(HG_HEADS):
        lb = lb_ref[:, h * HG_DIM:(h + 1) * HG_DIM]
        q, f, k, v, og = _hgrn_gates(zh, lb, h)
        g1, g2, g3 = _split3(jnp.log(f))
        b = (_dot(tri, g1) + (_dot(tri, g2) + _dot(tri, g3))) * LOG2_E
        diag = jnp.sum(q * k, axis=-1, keepdims=True)
        tiles = [[jnp.zeros((SUBLANES, LANES), F32)] * (c // LANES) for _ in range(c // SUBLANES)]
        for r in range(c // SUBLANES):
            cc = (r * SUBLANES) // LANES
            tiles[r][cc] = jnp.where(tile(lv, r, cc) == -1, diag[r * SUBLANES:(r + 1) * SUBLANES], 0.0)
        heads.append(dict(q=q, k=k, og=og, b=b, b3=b.reshape(c // SUBLANES, SUBLANES, HG_DIM), vb=v.astype(BF16),
                          a=tiles))

    for level in range(nlev):
        if level in between:
            between[level]()
        m = 1 << level
        groups = range(c // SUBLANES)
        upper = [r for r in groups if m < SUBLANES or (r * SUBLANES) & m]
        for hd in heads:
            w = jnp.exp2(-jnp.abs(hd["b"] - _level_reference(hd["b"], hd["b3"], sub, level, c)))
            if m >= SUBLANES:
                rg = lambda x, r: x[r * SUBLANES:(r + 1) * SUBLANES]
                scaled = jnp.concatenate([rg(hd["q"] if r in upper else hd["k"], r) for r in groups], axis=0) * w
                lhs = jnp.concatenate([rg(scaled, r) for r in upper], axis=0).astype(BF16)
                a_l = lax.dot_general(lhs, scaled.astype(BF16), _NT, preferred_element_type=F32)
            else:
                a_l = lax.dot_general((hd["q"] * w).astype(BF16), (hd["k"] * w).astype(BF16), _NT,
                                      preferred_element_type=F32)
            for i, r in enumerate(upper):
                base = (r * SUBLANES) & ~(2 * m - 1)
                if m >= LANES:
                    for cc in range(base // LANES, (base + m) // LANES):
                        hd["a"][r][cc] = tile(a_l, i, cc)
                else:
                    cc = base // LANES
                    hd["a"][r][cc] = jnp.where(tile(lv, r, cc) == level, tile(a_l, i, cc), hd["a"][r][cc])

    for h, hd in enumerate(heads):
        q, k, b, vb = hd["q"], hd["k"], hd["b"], hd["vb"]
        st = st_ref[h]
        a = jnp.concatenate([row[0] if len(row) == 1 else jnp.concatenate(row, axis=1) for row in hd["a"]], axis=0)
        o = _dot(a.astype(BF16), vb) + lax.dot_general(
            (q * jnp.exp2(b)).astype(BF16), st.astype(BF16), _NT, preferred_element_type=F32)
        b_end = b[c - 1:c, :]
        k_end = (k * jnp.exp2(b_end - b)).astype(BF16)
        st_new = st * jnp.exp2(b_end) + lax.dot_general(vb, k_end, _TN, preferred_element_type=F32)
        st_ref[h] = st_new
        o_ref[0, row0:row0 + c, h * HG_DIM:(h + 1) * HG_DIM] = (_rms(o, ng) * hd["og"]).astype(o_ref.dtype)


def _prompt_mix_kernel(x_ref, g_ref, wh_ref, wat_ref, cos_ref, sin_ref, lb_ref, ng_ref,
                       qt_ref, kt_ref, vt_ref, o_ref, s_out_ref, st_ref, lv_ref, *, chunk):
    ti = pl.program_id(1)
    tm = x_ref.shape[1]

    @pl.when(ti == 0)
    def _():
        st_ref[...] = jnp.zeros_like(st_ref)
        lv_ref[...] = _hgrn_level_table(chunk)

    h = _rms(x_ref[0], g_ref[...]).astype(BF16)
    zh = jnp.concatenate([_dot(h, wh_ref[:, p * HG_WIDTH:(p + 1) * HG_WIDTH]) for p in range(4)], axis=1)
    cos = cos_ref[...]
    sin = sin_ref[...]
    lv = lv_ref[...]
    ng = ng_ref[...]
    att_out = [(0, qt_ref), (1, kt_ref), (2, vt_ref)]
    n_chunks = tm // chunk
    per_chunk = -(-len(att_out) // n_chunks)
    nlev = chunk.bit_length() - 1
    for ci in range(n_chunks):
        mine, att_out = att_out[:per_chunk], att_out[per_chunk:]
        between = {(n * nlev) // per_chunk: functools.partial(_att_project, idx, h, wat_ref, cos, sin, ref)
                   for n, (idx, ref) in enumerate(mine)}
        _hgrn_chunk(zh[ci * chunk:(ci + 1) * chunk], lb_ref, ng, st_ref, lv, o_ref, ci * chunk, between)

    @pl.when(ti == pl.num_programs(1) - 1)
    def _():
        for hd in range(HG_HEADS):
            s_out_ref[0, hd] = st_ref[hd].T


def _prompt_mix(x3, g, wh_bf16, wat_bf16, cos_t, sin_t, lb, ng, tm, chunk):
    b, l, d = x3.shape
    const = lambda shape: pl.BlockSpec(shape, lambda i, j: (0, 0), pipeline_mode=pl.Buffered(1))
    tab = pl.BlockSpec((ROPE_HALF, tm), lambda i, j: (0, j))
    att = pl.BlockSpec((1, ATT_WIDTH, tm), lambda i, j: (i, 0, j))
    att_shape = jax.ShapeDtypeStruct((b, ATT_WIDTH, l), F32)
    return pl.pallas_call(
        functools.partial(_prompt_mix_kernel, chunk=chunk),
        grid=(b, l // tm),
        in_specs=[pl.BlockSpec((1, tm, d), lambda i, j: (i, j, 0)), const((1, d)), const(wh_bf16.shape),
                  const(wat_bf16.shape), tab, tab, const((1, HG_WIDTH)), const((1, HG_DIM))],
        out_specs=[att, att, att,
                   pl.BlockSpec((1, tm, HG_WIDTH), lambda i, j: (i, j, 0)),
                   pl.BlockSpec((1, HG_HEADS, HG_DIM, HG_DIM), lambda i, j: (i, 0, 0, 0))],
        out_shape=[att_shape, att_shape, att_shape,
                   jax.ShapeDtypeStruct((b, l, HG_WIDTH), BF16),
                   jax.ShapeDtypeStruct((b, HG_HEADS, HG_DIM, HG_DIM), F32)],
        scratch_shapes=[pltpu.VMEM((HG_HEADS, HG_DIM, HG_DIM), F32), pltpu.VMEM((chunk, chunk), I32)],
        compiler_params=pltpu.CompilerParams(dimension_semantics=("arbitrary", "arbitrary"),
                                             vmem_limit_bytes=VMEM_LIMIT),
        name="prompt_mix",
    )(x3, g, wh_bf16, wat_bf16, cos_t, sin_t, lb, ng)


HGRN_STEP_SEQS = 8


def _hgrn_step_kernel(zh_ref, lb_ref, ng_ref, s_ref, o_ref, s_out_ref):
    ng = ng_ref[...]
    eye = (lax.broadcasted_iota(I32, (HG_DIM, HG_DIM), 0) == lax.broadcasted_iota(I32, (HG_DIM, HG_DIM), 1))
    for n in range(zh_ref.shape[0]):
        zh = zh_ref[n]
        for h in range(HG_HEADS):
            lb = lb_ref[:, h * HG_DIM:(h + 1) * HG_DIM]
            q, f, k, v, og = _hgrn_gates(zh, lb, h)
            s_new = _column(f, eye) * s_ref[n, h] + _column(k, eye) * v
            s_out_ref[n, h] = s_new
            o = jnp.sum(_column(q, eye) * s_new, axis=0, keepdims=True)
            o_ref[n, :, h * HG_DIM:(h + 1) * HG_DIM] = _rms(o, ng) * og


def _hgrn_step(zh3, lb, ng, state):
    b = zh3.shape[0]
    ns = HGRN_STEP_SEQS if b % HGRN_STEP_SEQS == 0 else 1
    return pl.pallas_call(
        _hgrn_step_kernel,
        grid=(b // ns,),
        in_specs=[pl.BlockSpec((ns, 1, 4 * HG_WIDTH), lambda i: (i, 0, 0)),
                  pl.BlockSpec((1, HG_WIDTH), lambda i: (0, 0)),
                  pl.BlockSpec((1, HG_DIM), lambda i: (0, 0)),
                  pl.BlockSpec((ns, HG_HEADS, HG_DIM, HG_DIM), lambda i: (i, 0, 0, 0))],
        out_specs=[pl.BlockSpec((ns, 1, HG_WIDTH), lambda i: (i, 0, 0)),
                   pl.BlockSpec((ns, HG_HEADS, HG_DIM, HG_DIM), lambda i: (i, 0, 0, 0))],
        out_shape=[jax.ShapeDtypeStruct((b, 1, HG_WIDTH), F32),
                   jax.ShapeDtypeStruct(state.shape, F32)],
        compiler_params=pltpu.CompilerParams(dimension_semantics=("arbitrary",)),
        name="hgrn_step",
    )(zh3, lb, ng, state)


def _topk_rank(gm, axis, n_candidates, idx):
    rank = jnp.zeros(gm.shape, F32)
    for m in range(n_candidates):
        other = lax.slice_in_dim(gm, m, m + 1, axis=axis)
        tie = jnp.where(other == gm, jnp.where(idx > m, 1.0, 0.0), 0.0)
        rank = rank + jnp.where(other > gm, 1.0, tie)
    return rank


MOBA_Q_TILE = 128
MOBA_HEADS_PER_STEP = 2
MOBA_LOOKAHEAD = 3


def _moba_prompt_kernel(pt_ref, qt_ref, kt_ref, vt_ref, qs_ref, kc_ref, ot_ref, s_ref, sel_ref, kbuf_ref, sem_ref,
                        *, n_pages):
    step = pl.program_id(0) * pl.num_programs(1) + pl.program_id(1)
    slot = step % 2

    def page_copies(step_idx):
        return [pltpu.make_async_copy(kc_ref.at[0, pt_ref[step_idx, p]], kbuf_ref.at[step_idx % 2, p],
                                      sem_ref.at[step_idx % 2]) for p in range(n_pages)]

    @pl.when(step == 0)
    def _():
        for cp in page_copies(step):
            cp.start()

    @pl.when(step + 1 < pl.num_programs(0) * pl.num_programs(1))
    def _():
        for cp in page_copies(step + 1):
            cp.start()

    for cp in page_copies(step):
        cp.wait()

    eye = (lax.broadcasted_iota(I32, (ATT_DIM, ATT_DIM), 0) == lax.broadcasted_iota(I32, (ATT_DIM, ATT_DIM), 1))
    q_dec = qs_ref[0]
    q_cols = [_column(q_dec[h:h + 1, :], eye) for h in range(ATT_HEADS)]

    def score_page(p):
        rows = [jnp.sum(kbuf_ref[slot, p, h] * q_cols[h], axis=0, keepdims=True) for h in range(ATT_HEADS)]
        blk, half = divmod(p, PAGES_PER_BLOCK)
        s_ref[0, blk * ATT_HEADS:(blk + 1) * ATT_HEADS, half * PAGE_SIZE:(half + 1) * PAGE_SIZE] = (
            jnp.concatenate(rows, axis=0))

    l = qt_ref.shape[2]
    nb = l // MOBA_BLOCK
    tq = MOBA_Q_TILE
    scale = ATT_DIM ** -0.5
    blk_row = lax.broadcasted_iota(I32, (SUBLANES, l), 0)
    own_blk = lax.broadcasted_iota(I32, (SUBLANES, l), 1) // MOBA_BLOCK
    causal = {}
    for t0 in range(0, MOBA_BLOCK, tq):
        shape = (t0 + tq, tq)
        causal[t0] = lax.broadcasted_iota(I32, shape, 0) <= lax.broadcasted_iota(I32, shape, 1) + t0
    heads = []
    for hh in range(MOBA_HEADS_PER_STEP):
        chans = slice(hh * ATT_DIM, (hh + 1) * ATT_DIM)
        qt = qt_ref[0, chans, :]
        k_rows = kt_ref[0, chans, :].T
        kmean = jnp.sum(k_rows.reshape(nb, MOBA_BLOCK, ATT_DIM), axis=1) * (1.0 / MOBA_BLOCK)
        if nb < SUBLANES:
            kmean = jnp.concatenate([kmean, jnp.zeros((SUBLANES - nb, ATT_DIM), F32)], axis=0)
        gate = _dot_f32(kmean, qt, (((1,), (0,)), ((), ())))
        valid = blk_row < own_blk
        rank = _topk_rank(jnp.where(valid, gate, NEG_INF), 0, nb, blk_row)
        sel = jnp.where(valid, jnp.where(rank < MOBA_TOPK, 1.0, 0.0), 0.0)
        heads.append(dict(chans=chans, sel=sel, qtb=(qt * scale).astype(BF16), kb=k_rows.astype(BF16),
                          vtb=vt_ref[0, chans, :].astype(BF16)))

    tiles = [(hd, i, t0) for i in range(nb) for t0 in range(0, MOBA_BLOCK, tq) for hd in heads]

    def scores(tile):
        hd, i, t0 = tile
        q0 = i * MOBA_BLOCK + t0
        return _dot(hd["kb"][:q0 + tq], hd["qtb"][:, q0:q0 + tq])

    ahead = [scores(t) for t in tiles[:MOBA_LOOKAHEAD]]
    pages_done = 0
    for n, (hd, i, t0) in enumerate(tiles):
        s = ahead.pop(0)
        if n + MOBA_LOOKAHEAD < len(tiles):
            ahead.append(scores(tiles[n + MOBA_LOOKAHEAD]))
        pages_due = ((n + 1) * n_pages) // len(tiles)
        for p in range(pages_done, pages_due):
            score_page(p)
        pages_done = pages_due
        q0 = i * MOBA_BLOCK + t0
        pieces = [jnp.where(hd["sel"][j:j + 1, q0:q0 + tq] > 0.5, s[j * MOBA_BLOCK:(j + 1) * MOBA_BLOCK], NEG_INF)
                  for j in range(i)]
        pieces.append(jnp.where(causal[t0], s[i * MOBA_BLOCK:], NEG_INF))
        sm = pieces[0] if i == 0 else jnp.concatenate(pieces, axis=0)
        p = jnp.exp(sm - jnp.max(sm, axis=0, keepdims=True))
        den = jnp.sum(p, axis=0, keepdims=True)
        ot_ref[0, hd["chans"], q0:q0 + tq] = (_dot(hd["vtb"][:, :q0 + tq], p.astype(BF16)) / den).astype(ot_ref.dtype)

    nblk = n_pages // PAGES_PER_BLOCK
    lane = lax.broadcasted_iota(I32, (ATT_HEADS, LANES), 1)
    gate = jnp.full((ATT_HEADS, LANES), NEG_INF, F32)
    for n in range(nblk):
        g_n = jnp.sum(s_ref[0, n * ATT_HEADS:(n + 1) * ATT_HEADS, :], axis=1, keepdims=True) * (1.0 / MOBA_BLOCK)
        gate = jnp.where(lane == n, g_n, gate)
    rank = _topk_rank(gate, 1, nblk, lane)
    lane_f = lane.astype(F32)
    for j in range(MOBA_TOPK):
        pick = jnp.sum(jnp.where(rank == float(j), jnp.where(lane < nblk, lane_f, 0.0), 0.0), axis=1, keepdims=True)
        sel_ref[0, j] = pick.astype(I32)


def _moba_prompt(page_table, qt3, kt3, vt3, q_dec, cache_kt):
    b, _, l = qt3.shape
    dec, n_pages = page_table.shape
    assert l % MOBA_BLOCK == 0 and l // MOBA_BLOCK <= SUBLANES
    width = MOBA_HEADS_PER_STEP * ATT_DIM
    nj = ATT_WIDTH // width
    assert dec == b * nj, "one decode sequence is scored per prompt attention step"
    nblk = n_pages // PAGES_PER_BLOCK
    assert nblk <= LANES
    spec = pl.BlockSpec((1, width, l), lambda i, j, pt: (i, j, 0))
    per_step = lambda shape: pl.BlockSpec((1,) + shape, lambda i, j, pt: (i * nj + j,) + (0,) * len(shape))
    grid_spec = pltpu.PrefetchScalarGridSpec(
        num_scalar_prefetch=1,
        grid=(b, nj),
        in_specs=[spec, spec, spec, per_step((ATT_HEADS, ATT_DIM)), pl.BlockSpec(memory_space=pl.ANY)],
        out_specs=[spec, per_step((nblk * ATT_HEADS, MOBA_BLOCK)), per_step((MOBA_TOPK, ATT_HEADS, 1))],
        scratch_shapes=[pltpu.VMEM((2, n_pages, ATT_HEADS, ATT_DIM, PAGE_SIZE), F32),
                        pltpu.SemaphoreType.DMA((2,))],
    )
    o_att, scores, sel = pl.pallas_call(
        functools.partial(_moba_prompt_kernel, n_pages=n_pages),
        grid_spec=grid_spec,
        out_shape=[jax.ShapeDtypeStruct((b, ATT_WIDTH, l), BF16),
                   jax.ShapeDtypeStruct((dec, nblk * ATT_HEADS, MOBA_BLOCK), F32),
                   jax.ShapeDtypeStruct((dec, MOBA_TOPK, ATT_HEADS, 1), I32)],
        compiler_params=pltpu.CompilerParams(dimension_semantics=("arbitrary", "arbitrary"),
                                             vmem_limit_bytes=VMEM_LIMIT),
        name="moba_prompt",
    )(page_table, qt3, kt3, vt3, q_dec, cache_kt)
    return o_att, scores, sel.reshape(dec, MOBA_TOPK, ATT_HEADS)


ATT_PAGES = MOBA_TOPK * PAGES_PER_BLOCK


def _moba_attend_kernel(pt_ref, sel_ref, s_ref, q_ref, kn_ref, vn_ref, vc_ref, o_ref, vbuf, sem_ref):
    bi = pl.program_id(0)
    slot = bi % 2
    scale = ATT_DIM ** -0.5

    def copies(b_idx, slot_idx):
        out = []
        for h in range(ATT_HEADS):
            for j in range(MOBA_TOPK):
                blk = sel_ref[b_idx, j, h]
                for p in range(PAGES_PER_BLOCK):
                    page = pt_ref[b_idx, blk * PAGES_PER_BLOCK + p]
                    out.append(pltpu.make_async_copy(vc_ref.at[0, page, h],
                                                     vbuf.at[slot_idx, h, j * PAGES_PER_BLOCK + p],
                                                     sem_ref.at[slot_idx]))
        return out

    @pl.when(bi == 0)
    def _():
        for cp in copies(bi, slot):
            cp.start()

    @pl.when(bi + 1 < pl.num_programs(0))
    def _():
        for cp in copies(bi + 1, 1 - slot):
            cp.start()

    for cp in copies(bi, slot):
        cp.wait()

    eye = (lax.broadcasted_iota(I32, (ATT_DIM, LANES), 0) == lax.broadcasted_iota(I32, (ATT_DIM, LANES), 1))
    s_own_all = jnp.sum(q_ref[0] * kn_ref[0], axis=1, keepdims=True) * scale
    for h in range(ATT_HEADS):
        s_blk = [s_ref[0, pl.ds(sel_ref[bi, j, h] * ATT_HEADS + h, 1), :] * scale for j in range(MOBA_TOPK)]
        s_own = s_own_all[h:h + 1, :]
        mx = s_own
        for s_j in s_blk:
            mx = jnp.maximum(mx, jnp.max(s_j, axis=1, keepdims=True))
        p_blk = [jnp.exp(s_j - mx) for s_j in s_blk]
        p_own = jnp.exp(s_own - mx)
        den = p_own
        acc = jnp.zeros((ATT_DIM, PAGE_SIZE), F32)
        for j, p_j in enumerate(p_blk):
            den = den + jnp.sum(p_j, axis=1, keepdims=True)
            for p in range(PAGES_PER_BLOCK):
                acc = acc + p_j[:, p * PAGE_SIZE:(p + 1) * PAGE_SIZE] * vbuf[slot, h, j * PAGES_PER_BLOCK + p]
        o_col = jnp.sum(acc, axis=1, keepdims=True)
        o_row = jnp.sum(jnp.where(eye, o_col, 0.0), axis=0, keepdims=True)[:, :ATT_DIM]
        o_ref[0, h:h + 1, :] = (o_row + p_own * vn_ref[0, h:h + 1, :]) / den


def _moba_attend(page_table, sel, scores, q3, kn3, vn3, cache_vt):
    b = page_table.shape[0]
    row = pl.BlockSpec((1, ATT_HEADS, ATT_DIM), lambda i, pt, sl: (i, 0, 0))
    grid_spec = pltpu.PrefetchScalarGridSpec(
        num_scalar_prefetch=2,
        grid=(b,),
        in_specs=[pl.BlockSpec((1,) + scores.shape[1:], lambda i, pt, sl: (i, 0, 0)),
                  row, row, row, pl.BlockSpec(memory_space=pl.ANY)],
        out_specs=row,
        scratch_shapes=[pltpu.VMEM((2, ATT_HEADS, ATT_PAGES, ATT_DIM, PAGE_SIZE), F32),
                        pltpu.SemaphoreType.DMA((2,))],
    )
    return pl.pallas_call(
        _moba_attend_kernel,
        grid_spec=grid_spec,
        out_shape=jax.ShapeDtypeStruct((b, ATT_HEADS, ATT_DIM), F32),
        compiler_params=pltpu.CompilerParams(dimension_semantics=("arbitrary",), vmem_limit_bytes=VMEM_LIMIT),
        name="moba_attend",
    )(page_table, sel, scores, q3, kn3, vn3, cache_vt)


FFN_COLS = 256
FFN_LOOKAHEAD = 1


def _post_kernel(x_ref, ohg_ref, oatt_ref, prev_ref, wo_ref, gpm_ref, gpf_ref, gqf_ref, wi_ref, cw_ref, cb_ref,
                 wf_ref, y_ref, conv_ref, gate_ref, act_ref, *, decode):
    li = pl.program_id(1)
    tl = x_ref.shape[1]
    d_ff = wf_ref.shape[0]
    if decode:
        att = _dot(oatt_ref[0].astype(BF16), wo_ref[HG_WIDTH:, :])
    else:
        att = lax.dot_general(oatt_ref[0].astype(BF16), wo_ref[HG_WIDTH:, :], _TN, preferred_element_type=F32)
    mix = _dot(ohg_ref[0].astype(BF16), wo_ref[:HG_WIDTH, :]) + att
    x1 = x_ref[0] + _rms(mix, gpm_ref[...])
    h2 = _rms(x1, gpf_ref[...]).astype(BF16)

    if not decode:
        @pl.when(li == 0)
        def _():
            gate_ref[0:SUBLANES, :] = jnp.zeros((SUBLANES, d_ff), F32)
            gate_ref[SUBLANES - 2:SUBLANES - 1, :] = prev_ref[0, 0]
            gate_ref[SUBLANES - 1:SUBLANES, :] = prev_ref[0, 1]

    def project(c0):
        return (_dot(h2, wi_ref[:, c0:c0 + FFN_COLS]), _dot(h2, wi_ref[:, d_ff + c0:d_ff + c0 + FFN_COLS]))

    chunks = list(range(0, d_ff, FFN_COLS))
    ahead = [project(c0) for c0 in chunks[:FFN_LOOKAHEAD]]
    for n, c0 in enumerate(chunks):
        cols = slice(c0, c0 + FFN_COLS)
        gate, up = ahead.pop(0)
        if n + FFN_LOOKAHEAD < len(chunks):
            ahead.append(project(chunks[n + FFN_LOOKAHEAD]))
        w0 = cw_ref[0:1, cols]
        w1 = cw_ref[1:2, cols]
        w2 = cw_ref[2:3, cols]
        if decode:
            conv = prev_ref[0, 0, :, cols] * w0 + prev_ref[0, 1, :, cols] * w1 + gate * w2 + cb_ref[:, cols]
            conv_ref[0, 0, :, cols] = prev_ref[0, 1, :, cols]
            conv_ref[0, 1, :, cols] = gate
        else:
            gate_ref[SUBLANES:SUBLANES + tl, cols] = gate
            conv = (gate_ref[SUBLANES - 2:SUBLANES - 2 + tl, cols] * w0
                    + gate_ref[SUBLANES - 1:SUBLANES - 1 + tl, cols] * w1 + gate * w2 + cb_ref[:, cols])
        act_ref[:, cols] = (jax.nn.gelu(conv, approximate=True) * up).astype(BF16)
    y_ref[0] = x1 + _rms(_dot(act_ref[...], wf_ref[...]), gqf_ref[...])

    if not decode:
        last = gate_ref[tl:tl + SUBLANES, :]
        gate_ref[0:SUBLANES, :] = last

        @pl.when(li == pl.num_programs(1) - 1)
        def _():
            conv_ref[0, 0] = last[SUBLANES - 2:SUBLANES - 1, :]
            conv_ref[0, 1] = last[SUBLANES - 1:SUBLANES, :]


def _post(x3, ohg3, oatt3, prev, wo, gpm, gpf, gqf, wi, cw, cb, wf, tl, decode):
    b, l, d = x3.shape
    d_ff = wf.shape[0]
    row = lambda width: pl.BlockSpec((1, tl, width), lambda i, j: (i, j, 0))
    const = lambda shape: pl.BlockSpec(shape, lambda i, j: (0,) * len(shape), pipeline_mode=pl.Buffered(1))
    if decode:
        state = pl.BlockSpec((1, CONV_WIDTH - 1, tl, d_ff), lambda i, j: (i, 0, j, 0))
        att = row(ATT_WIDTH)
    else:
        state = pl.BlockSpec((1, CONV_WIDTH - 1, 1, d_ff), lambda i, j: (i, 0, 0, 0))
        att = pl.BlockSpec((1, ATT_WIDTH, tl), lambda i, j: (i, 0, j))
    return pl.pallas_call(
        functools.partial(_post_kernel, decode=decode),
        grid=(b, l // tl),
        in_specs=[row(d), row(HG_WIDTH), att, state,
                  const(wo.shape), const((1, d)), const((1, d)), const((1, d)),
                  const(wi.shape), const(cw.shape), const((1, d_ff)), const(wf.shape)],
        out_specs=[row(d), state],
        out_shape=[jax.ShapeDtypeStruct((b, l, d), F32),
                   jax.ShapeDtypeStruct(prev.shape, F32)],
        scratch_shapes=[pltpu.VMEM((tl + SUBLANES, d_ff), F32), pltpu.VMEM((tl, d_ff), BF16)],
        compiler_params=pltpu.CompilerParams(dimension_semantics=("arbitrary", "arbitrary"),
                                             vmem_limit_bytes=VMEM_LIMIT),
        name="post_decode" if decode else "post_prompt",
    )(x3, ohg3, oatt3, prev, wo, gpm, gpf, gqf, wi, cw, cb, wf)


INPROJ_ROWS = 512
POST_ROWS = 512
HGRN_CHUNK = 128


def kernel(x_prompt, x_sample, cache_k, cache_v, state_hgrn, state_conv, page_table, w_in, w_out, hg_lb_logits,
           hg_norm, g_pre_mix, g_post_mix, g_pre_ffn, g_post_ffn, w_ffn_in, conv_w, conv_b, w_ffn_out):
    depth = w_in.shape[0]
    assert depth == 1, "single-layer step"
    bp, seq, d = x_prompt.shape
    bs, t_dec, _ = x_sample.shape
    assert t_dec == 1
    n_pages = page_table.shape[1]
    past = n_pages * PAGE_SIZE
    lb_all = jnp.cumsum(jax.nn.softmax(hg_lb_logits.astype(F32), axis=0), axis=0)

    layer = 0
    lb = lb_all[layer][None, :]
    ng = hg_norm[layer][None, :]
    w_in_b = w_in[layer].astype(BF16)
    w_out_b = w_out[layer].astype(BF16)
    w_fi_b = w_ffn_in[layer].astype(BF16)
    w_fo_b = w_ffn_out[layer].astype(BF16)
    gains = [g[layer][None, :] for g in (g_pre_mix, g_post_mix, g_pre_ffn, g_post_ffn)]
    cw = conv_w[layer]
    cb = conv_b[layer][None, :]
    post_w = (w_out_b, gains[1], gains[2], gains[3], w_fi_b, cw, cb, w_fo_b)

    tabs_s = _rope_tables(jnp.full((bs,), past, dtype=I32))
    zh_s, q_s, k_s, v_s = _inproj(x_sample.reshape(bs, d), gains[0], w_in_b, tabs_s, bs)
    heads = lambda a: a.reshape(bs, ATT_HEADS, ATT_DIM)
    cache_kt = jnp.transpose(cache_k, (0, 1, 3, 4, 2))
    cache_vt = jnp.transpose(cache_v, (0, 1, 3, 4, 2))

    tm = min(INPROJ_ROWS, seq)
    cos_t, sin_t = _rope_tables_t(jnp.arange(seq, dtype=I32))
    w_h = w_in_b[:, :4 * HG_WIDTH]
    w_at = w_in_b[:, 4 * HG_WIDTH:].T
    qt, kt, vt, o_hg, s_p = _prompt_mix(x_prompt, gains[0], w_h, w_at, cos_t, sin_t, lb, ng, tm,
                                        min(HGRN_CHUNK, seq))
    o_att, scores, sel = _moba_prompt(page_table, qt, kt, vt, heads(q_s), cache_kt)
    d_ff = w_ffn_out.shape[1]
    zero_prev = jnp.zeros((bp, CONV_WIDTH - 1, 1, d_ff), F32)
    y_p, conv_p = _post(x_prompt, o_hg, o_att, zero_prev, *post_w, tl=min(POST_ROWS, seq), decode=False)
    conv_p = conv_p.reshape(bp, CONV_WIDTH - 1, d_ff)

    o_hg_s, s_s = _hgrn_step(zh_s.reshape(bs, 1, 4 * HG_WIDTH), lb, ng, state_hgrn[layer])
    o_att_s = _moba_attend(page_table, sel, scores, heads(q_s), heads(k_s), heads(v_s), cache_vt)
    rows3 = lambda a: a.reshape(1, bs, a.shape[-1])
    prev_s = jnp.transpose(state_conv[layer], (1, 0, 2))[None]
    y_s, conv_s = _post(rows3(x_sample), rows3(o_hg_s), o_att_s.reshape(1, bs, ATT_WIDTH), prev_s, *post_w,
                        tl=bs, decode=True)
    y_s = y_s.reshape(bs, 1, d)
    conv_s = jnp.transpose(conv_s[0], (1, 0, 2))

    kv_p = lambda a: jnp.transpose(a.reshape(1, bp, ATT_HEADS, ATT_DIM, seq), (0, 1, 4, 2, 3))
    kv_s = lambda a: a.reshape(1, bs, 1, ATT_HEADS, ATT_DIM)
    return (y_p, y_s, kv_p(kt), kv_p(vt), s_p[None], conv_p[None],
            kv_s(k_s), kv_s(v_s), s_s[None], conv_s[None])
```

```python
import functools

import jax
import jax.numpy as jnp
from jax import lax
from jax.experimental import pallas as pl
from jax.experimental.pallas import tpu as pltpu

F32 = jnp.float32
BF16 = jnp.bfloat16
I32 = jnp.int32

HG_HEADS = 4
HG_DIM = 128
HG_WIDTH = HG_HEADS * HG_DIM
ATT_HEADS = 8
ATT_DIM = 64
ATT_WIDTH = ATT_HEADS * ATT_DIM
MOBA_BLOCK = 256
MOBA_TOPK = 3
ROPE_THETA = 500000.0
ROPE_DIMS = ATT_DIM // 4
ROPE_HALF = ROPE_DIMS // 2
CONV_WIDTH = 3
NORM_EPS = 1e-6
PAGE_SIZE = 128
PAGES_PER_BLOCK = MOBA_BLOCK // PAGE_SIZE

LANES = 128
SUBLANES = 8
VMEM_LIMIT = 56 * 1024 * 1024

NEG_INF = float("-inf")
LOG2_E = 1.4426950408889634
_NT = (((1,), (1,)), ((), ()))
_TN = (((0,), (0,)), ((), ()))


def _dot(a, b):
    return jnp.dot(a, b, preferred_element_type=F32)


def _rms(x, g):
    return x * lax.rsqrt(jnp.mean(x * x, axis=-1, keepdims=True) + NORM_EPS) * g


def _sigmoid(x):
    return 0.5 * jnp.tanh(0.5 * x) + 0.5


def _column(row, eye):
    return jnp.sum(jnp.where(eye, row, 0.0), axis=1, keepdims=True)


def _split3(x):
    x1 = x.astype(BF16)
    r1 = x - x1.astype(F32)
    x2 = r1.astype(BF16)
    r2 = r1 - x2.astype(F32)
    return x1, x2, r2.astype(BF16)


def _dot_f32(a, b, dims):
    a1, a2, a3 = _split3(a)
    b1, b2, b3 = _split3(b)
    dg = functools.partial(lax.dot_general, dimension_numbers=dims, preferred_element_type=F32)
    small = dg(a1, b3) + dg(a3, b1) + dg(a2, b2)
    mid = dg(a1, b2) + dg(a2, b1)
    return dg(a1, b1) + (mid + small)


def _inproj_kernel(x_ref, g_ref, w_ref, cos_ref, sa_ref, sb_ref, zh_ref, q_ref, k_ref, v_ref):
    h = _rms(x_ref[...], g_ref[...]).astype(BF16)
    hgw = 4 * HG_WIDTH
    for c in range(4):
        zh_ref[:, c * HG_WIDTH:(c + 1) * HG_WIDTH] = _dot(h, w_ref[:, c * HG_WIDTH:(c + 1) * HG_WIDTH])
    cos = cos_ref[...]
    sa = sa_ref[...]
    sb = sb_ref[...]
    for out_ref, off in ((q_ref, hgw), (k_ref, hgw + ATT_WIDTH)):
        a = _dot(h, w_ref[:, off:off + ATT_WIDTH])
        for j in range(ATT_WIDTH // LANES):
            aj = a[:, j * LANES:(j + 1) * LANES]
            out_ref[:, j * LANES:(j + 1) * LANES] = (
                aj * cos + pltpu.roll(aj, LANES - ROPE_HALF, 1) * sa + pltpu.roll(aj, ROPE_HALF, 1) * sb)
    v_ref[...] = _dot(h, w_ref[:, hgw + 2 * ATT_WIDTH:hgw + 3 * ATT_WIDTH])


def _rope_tables(pos):
    inv = jnp.power(jnp.float32(ROPE_THETA), -jnp.arange(ROPE_HALF, dtype=F32) * (2.0 / ROPE_DIMS))
    ang = pos.astype(F32)[:, None] * inv[None, :]
    cos = jnp.cos(ang)
    sin = jnp.sin(ang)
    n = pos.shape[0]
    rest = ATT_DIM - ROPE_DIMS
    cos_h = jnp.concatenate([cos, cos, jnp.ones((n, rest), F32)], axis=1)
    sa_h = jnp.concatenate([-sin, jnp.zeros((n, ATT_DIM - ROPE_HALF), F32)], axis=1)
    sb_h = jnp.concatenate([jnp.zeros((n, ROPE_HALF), F32), sin, jnp.zeros((n, rest), F32)], axis=1)
    rep = LANES // ATT_DIM
    return tuple(jnp.tile(t, (1, rep)) for t in (cos_h, sa_h, sb_h))


def _rope_tables_t(pos):
    inv = jnp.power(jnp.float32(ROPE_THETA), -jnp.arange(ROPE_HALF, dtype=F32) * (2.0 / ROPE_DIMS))
    ang = pos.astype(F32)[:, None] * inv[None, :]
    return jnp.cos(ang).T, jnp.sin(ang).T


def _inproj(x2d, g, w_bf16, tables, tm):
    rows, d = x2d.shape
    ncols = w_bf16.shape[1]
    npos = tables[0].shape[0] // tm
    tab_spec = pl.BlockSpec((tm, LANES), lambda i: (i % npos, 0))
    row_spec = lambda width: pl.BlockSpec((tm, width), lambda i: (i, 0))
    return pl.pallas_call(
        _inproj_kernel,
        grid=(rows // tm,),
        in_specs=[row_spec(d),
                  pl.BlockSpec((1, d), lambda i: (0, 0)),
                  pl.BlockSpec((d, ncols), lambda i: (0, 0), pipeline_mode=pl.Buffered(1)),
                  tab_spec, tab_spec, tab_spec],
        out_specs=[row_spec(4 * HG_WIDTH), row_spec(ATT_WIDTH), row_spec(ATT_WIDTH), row_spec(ATT_WIDTH)],
        out_shape=[jax.ShapeDtypeStruct((rows, 4 * HG_WIDTH), F32),
                   jax.ShapeDtypeStruct((rows, ATT_WIDTH), F32),
                   jax.ShapeDtypeStruct((rows, ATT_WIDTH), F32),
                   jax.ShapeDtypeStruct((rows, ATT_WIDTH), F32)],
        compiler_params=pltpu.CompilerParams(dimension_semantics=("arbitrary",), vmem_limit_bytes=VMEM_LIMIT),
        name="inproj",
    )(x2d, g, w_bf16, *tables)


def _att_project(idx, h, wat_ref, cos, sin, out_ref):
    a = lax.dot_general(wat_ref[idx * ATT_WIDTH:(idx + 1) * ATT_WIDTH, :], h, _NT,
                        preferred_element_type=F32)
    out_ref[0] = a
    if idx == 2:
        return
    for hd in range(ATT_HEADS):
        r = hd * ATT_DIM
        x1 = a[r:r + ROPE_HALF]
        x2 = a[r + ROPE_HALF:r + ROPE_DIMS]
        out_ref[0, r:r + ROPE_HALF, :] = x1 * cos - x2 * sin
        out_ref[0, r + ROPE_HALF:r + ROPE_DIMS, :] = x2 * cos + x1 * sin


def _hgrn_gates(zh_row, lb, h):
    sl = lambda part: slice(part * HG_WIDTH + h * HG_DIM, part * HG_WIDTH + (h + 1) * HG_DIM)
    hq = zh_row[:, sl(0)]
    hf = zh_row[:, sl(1)]
    v = zh_row[:, sl(2)]
    hg = zh_row[:, sl(3)]
    q = hq * _sigmoid(hq)
    f = lb + (1.0 - lb) * _sigmoid(hf)
    return q, f, 1.0 - f, v, hg * _sigmoid(hg)


def _level_reference(b, b3, sub, level, c):
    m = 1 << level
    if m >= SUBLANES:
        pieces = []
        for grp in range(c // (2 * m)):
            r = grp * 2 * m + m - 1
            pieces.append(jnp.broadcast_to(b[r:r + 1, :], (2 * m, HG_DIM)))
        return pieces[0] if len(pieces) == 1 else jnp.concatenate(pieces, axis=0)
    beta3 = None
    for g0 in range(0, SUBLANES, 2 * m):
        r = g0 + m - 1
        piece = jnp.broadcast_to(b3[:, r:r + 1, :], b3.shape)
        beta3 = piece if beta3 is None else jnp.where(sub >= g0, piece, beta3)
    return beta3.reshape(c, HG_DIM)


def _hgrn_level_table(c):
    rows = lax.broadcasted_iota(I32, (c, c), 0)
    cols = lax.broadcasted_iota(I32, (c, c), 1)
    x = rows ^ cols
    lev = jnp.zeros((c, c), I32)
    for j in range(1, c.bit_length() - 1):
        lev = lev + jnp.where(x >= (1 << j), 1, 0)
    return jnp.where(cols < rows, lev, jnp.where(cols == rows, -1, -2))


def _hgrn_chunk(zh, lb_ref, ng, st_ref, lv, o_ref, row0, between):
    c = zh.shape[0]
    nlev = c.bit_length() - 1
    tri = jnp.where(lax.broadcasted_iota(I32, (c, c), 1) <= lax.broadcasted_iota(I32, (c, c), 0), 1.0, 0.0).astype(BF16)
    sub = lax.broadcasted_iota(I32, (c // SUBLANES, SUBLANES, HG_DIM), 1)

    def tile(x, r, cc):
        return x[r * SUBLANES:(r + 1) * SUBLANES, cc * LANES:(cc + 1) * LANES]

    heads = []
    for h in range(HG_HEADS):
        lb = lb_ref[:, h * HG_DIM:(h + 1) * HG_DIM]
        q, f, k, v, og = _hgrn_gates(zh, lb, h)
        g1, g2, g3 = _split3(jnp.log(f))
        b = (_dot(tri, g1) + (_dot(tri, g2) + _dot(tri, g3))) * LOG2_E
        diag = jnp.sum(q * k, axis=-1, keepdims=True)
        tiles = [[jnp.zeros((SUBLANES, LANES), F32)] * (c // LANES) for _ in range(c // SUBLANES)]
        for r in range(c // SUBLANES):
            cc = (r * SUBLANES) // LANES
            tiles[r][cc] = jnp.where(tile(lv, r, cc) == -1, diag[r * SUBLANES:(r + 1) * SUBLANES], 0.0)
        heads.append(dict(q=q, k=k, og=og, b=b, b3=b.reshape(c // SUBLANES, SUBLANES, HG_DIM), vb=v.astype(BF16),
                          a=tiles))

    for level in range(nlev):
        if level in between:
            between[level]()
        m = 1 << level
        groups = range(c // SUBLANES)
        upper = [r for r in groups if m < SUBLANES or (r * SUBLANES) & m]
        for hd in heads:
            w = jnp.exp2(-jnp.abs(hd["b"] - _level_reference(hd["b"], hd["b3"], sub, level, c)))
            if m >= SUBLANES:
                rg = lambda x, r: x[r * SUBLANES:(r + 1) * SUBLANES]
                scaled = jnp.concatenate([rg(hd["q"] if r in upper else hd["k"], r) for r in groups], axis=0) * w
                lhs = jnp.concatenate([rg(scaled, r) for r in upper], axis=0).astype(BF16)
                a_l = lax.dot_general(lhs, scaled.astype(BF16), _NT, preferred_element_type=F32)
            else:
                a_l = lax.dot_general((hd["q"] * w).astype(BF16), (hd["k"] * w).astype(BF16), _NT,
                                      preferred_element_type=F32)
            for i, r in enumerate(upper):
                base = (r * SUBLANES) & ~(2 * m - 1)
                if m >= LANES:
                    for cc in range(base // LANES, (base + m) // LANES):
                        hd["a"][r][cc] = tile(a_l, i, cc)
                else:
                    cc = base // LANES
                    hd["a"][r][cc] = jnp.where(tile(lv, r, cc) == level, tile(a_l, i, cc), hd["a"][r][cc])

    for h, hd in enumerate(heads):
        q, k, b, vb = hd["q"], hd["k"], hd["b"], hd["vb"]
        st = st_ref[h]
        a = jnp.concatenate([row[0] if len(row) == 1 else jnp.concatenate(row, axis=1) for row in hd["a"]], axis=0)
        o = _dot(a.astype(BF16), vb) + lax.dot_general(
            (q * jnp.exp2(b)).astype(BF16), st.astype(BF16), _NT, preferred_element_type=F32)
        b_end = b[c - 1:c, :]
        k_end = (k * jnp.exp2(b_end - b)).astype(BF16)
        st_new = st * jnp.exp2(b_end) + lax.dot_general(vb, k_end, _TN, preferred_element_type=F32)
        st_ref[h] = st_new
        o_ref[0, row0:row0 + c, h * HG_DIM:(h + 1) * HG_DIM] = (_rms(o, ng) * hd["og"]).astype(o_ref.dtype)


def _prompt_mix_kernel(x_ref, g_ref, wh_ref, wat_ref, cos_ref, sin_ref, lb_ref, ng_ref,
                       qt_ref, kt_ref, vt_ref, o_ref, s_out_ref, st_ref, lv_ref, *, chunk):
    ti = pl.program_id(1)
    tm = x_ref.shape[1]

    @pl.when(ti == 0)
    def _():
        st_ref[...] = jnp.zeros_like(st_ref)
        lv_ref[...] = _hgrn_level_table(chunk)

    h = _rms(x_ref[0], g_ref[...]).astype(BF16)
    zh = jnp.concatenate([_dot(h, wh_ref[:, p * HG_WIDTH:(p + 1) * HG_WIDTH]) for p in range(4)], axis=1)
    cos = cos_ref[...]
    sin = sin_ref[...]
    lv = lv_ref[...]
    ng = ng_ref[...]
    att_out = [(0, qt_ref), (1, kt_ref), (2, vt_ref)]
    n_chunks = tm // chunk
    per_chunk = -(-len(att_out) // n_chunks)
    nlev = chunk.bit_length() - 1
    for ci in range(n_chunks):
        mine, att_out = att_out[:per_chunk], att_out[per_chunk:]
        between = {(n * nlev) // per_chunk: functools.partial(_att_project, idx, h, wat_ref, cos, sin, ref)
                   for n, (idx, ref) in enumerate(mine)}
        _hgrn_chunk(zh[ci * chunk:(ci + 1) * chunk], lb_ref, ng, st_ref, lv, o_ref, ci * chunk, between)

    @pl.when(ti == pl.num_programs(1) - 1)
    def _():
        for hd in range(HG_HEADS):
            s_out_ref[0, hd] = st_ref[hd].T


def _prompt_mix(x3, g, wh_bf16, wat_bf16, cos_t, sin_t, lb, ng, tm, chunk):
    b, l, d = x3.shape
    const = lambda shape: pl.BlockSpec(shape, lambda i, j: (0, 0), pipeline_mode=pl.Buffered(1))
    tab = pl.BlockSpec((ROPE_HALF, tm), lambda i, j: (0, j))
    att = pl.BlockSpec((1, ATT_WIDTH, tm), lambda i, j: (i, 0, j))
    att_shape = jax.ShapeDtypeStruct((b, ATT_WIDTH, l), F32)
    return pl.pallas_call(
        functools.partial(_prompt_mix_kernel, chunk=chunk),
        grid=(b, l // tm),
        in_specs=[pl.BlockSpec((1, tm, d), lambda i, j: (i, j, 0)), const((1, d)), const(wh_bf16.shape),
                  const(wat_bf16.shape), tab, tab, const((1, HG_WIDTH)), const((1, HG_DIM))],
        out_specs=[att, att, att,
                   pl.BlockSpec((1, tm, HG_WIDTH), lambda i, j: (i, j, 0)),
                   pl.BlockSpec((1, HG_HEADS, HG_DIM, HG_DIM), lambda i, j: (i, 0, 0, 0))],
        out_shape=[att_shape, att_shape, att_shape,
                   jax.ShapeDtypeStruct((b, l, HG_WIDTH), BF16),
                   jax.ShapeDtypeStruct((b, HG_HEADS, HG_DIM, HG_DIM), F32)],
        scratch_shapes=[pltpu.VMEM((HG_HEADS, HG_DIM, HG_DIM), F32), pltpu.VMEM((chunk, chunk), I32)],
        compiler_params=pltpu.CompilerParams(dimension_semantics=("arbitrary", "arbitrary"),
                                             vmem_limit_bytes=VMEM_LIMIT),
        name="prompt_mix",
    )(x3, g, wh_bf16, wat_bf16, cos_t, sin_t, lb, ng)


HGRN_STEP_SEQS = 8


def _hgrn_step_kernel(zh_ref, lb_ref, ng_ref, s_ref, o_ref, s_out_ref):
    ng = ng_ref[...]
    eye = (lax.broadcasted_iota(I32, (HG_DIM, HG_DIM), 0) == lax.broadcasted_iota(I32, (HG_DIM, HG_DIM), 1))
    for n in range(zh_ref.shape[0]):
        zh = zh_ref[n]
        for h in range(HG_HEADS):
            lb = lb_ref[:, h * HG_DIM:(h + 1) * HG_DIM]
            q, f, k, v, og = _hgrn_gates(zh, lb, h)
            s_new = _column(f, eye) * s_ref[n, h] + _column(k, eye) * v
            s_out_ref[n, h] = s_new
            o = jnp.sum(_column(q, eye) * s_new, axis=0, keepdims=True)
            o_ref[n, :, h * HG_DIM:(h + 1) * HG_DIM] = _rms(o, ng) * og


def _hgrn_step(zh3, lb, ng, state):
    b = zh3.shape[0]
    ns = HGRN_STEP_SEQS if b % HGRN_STEP_SEQS == 0 else 1
    return pl.pallas_call(
        _hgrn_step_kernel,
        grid=(b // ns,),
        in_specs=[pl.BlockSpec((ns, 1, 4 * HG_WIDTH), lambda i: (i, 0, 0)),
                  pl.BlockSpec((1, HG_WIDTH), lambda i: (0, 0)),
                  pl.BlockSpec((1, HG_DIM), lambda i: (0, 0)),
                  pl.BlockSpec((ns, HG_HEADS, HG_DIM, HG_DIM), lambda i: (i, 0, 0, 0))],
        out_specs=[pl.BlockSpec((ns, 1, HG_WIDTH), lambda i: (i, 0, 0)),
                   pl.BlockSpec((ns, HG_HEADS, HG_DIM, HG_DIM), lambda i: (i, 0, 0, 0))],
        out_shape=[jax.ShapeDtypeStruct((b, 1, HG_WIDTH), F32),
                   jax.ShapeDtypeStruct(state.shape, F32)],
        compiler_params=pltpu.CompilerParams(dimension_semantics=("arbitrary",)),
        name="hgrn_step",
    )(zh3, lb, ng, state)


def _topk_rank(gm, axis, n_candidates, idx):
    rank = jnp.zeros(gm.shape, F32)
    for m in range(n_candidates):
        other = lax.slice_in_dim(gm, m, m + 1, axis=axis)
        tie = jnp.where(other == gm, jnp.where(idx > m, 1.0, 0.0), 0.0)
        rank = rank + jnp.where(other > gm, 1.0, tie)
    return rank


MOBA_Q_TILE = 128
MOBA_HEADS_PER_STEP = 2
MOBA_LOOKAHEAD = 3


def _moba_prompt_kernel(pt_ref, qt_ref, kt_ref, vt_ref, qs_ref, kc_ref, ot_ref, s_ref, sel_ref, kbuf_ref, sem_ref,
                        *, n_pages):
    step = pl.program_id(0) * pl.num_programs(1) + pl.program_id(1)
    slot = step % 2

    def page_copies(step_idx):
        return [pltpu.make_async_copy(kc_ref.at[0, pt_ref[step_idx, p]], kbuf_ref.at[step_idx % 2, p],
                                      sem_ref.at[step_idx % 2]) for p in range(n_pages)]

    @pl.when(step == 0)
    def _():
        for cp in page_copies(step):
            cp.start()

    @pl.when(step + 1 < pl.num_programs(0) * pl.num_programs(1))
    def _():
        for cp in page_copies(step + 1):
            cp.start()

    for cp in page_copies(step):
        cp.wait()

    eye = (lax.broadcasted_iota(I32, (ATT_DIM, ATT_DIM), 0) == lax.broadcasted_iota(I32, (ATT_DIM, ATT_DIM), 1))
    q_dec = qs_ref[0]
    q_cols = [_column(q_dec[h:h + 1, :], eye) for h in range(ATT_HEADS)]

    def score_page(p):
        rows = [jnp.sum(kbuf_ref[slot, p, h] * q_cols[h], axis=0, keepdims=True) for h in range(ATT_HEADS)]
        blk, half = divmod(p, PAGES_PER_BLOCK)
        s_ref[0, blk * ATT_HEADS:(blk + 1) * ATT_HEADS, half * PAGE_SIZE:(half + 1) * PAGE_SIZE] = (
            jnp.concatenate(rows, axis=0))

    l = qt_ref.shape[2]
    nb = l // MOBA_BLOCK
    tq = MOBA_Q_TILE
    scale = ATT_DIM ** -0.5
    blk_row = lax.broadcasted_iota(I32, (SUBLANES, l), 0)
    own_blk = lax.broadcasted_iota(I32, (SUBLANES, l), 1) // MOBA_BLOCK
    causal = {}
    for t0 in range(0, MOBA_BLOCK, tq):
        shape = (t0 + tq, tq)
        causal[t0] = lax.broadcasted_iota(I32, shape, 0) <= lax.broadcasted_iota(I32, shape, 1) + t0
    heads = []
    for hh in range(MOBA_HEADS_PER_STEP):
        chans = slice(hh * ATT_DIM, (hh + 1) * ATT_DIM)
        qt = qt_ref[0, chans, :]
        k_rows = kt_ref[0, chans, :].T
        kmean = jnp.sum(k_rows.reshape(nb, MOBA_BLOCK, ATT_DIM), axis=1) * (1.0 / MOBA_BLOCK)
        if nb < SUBLANES:
            kmean = jnp.concatenate([kmean, jnp.zeros((SUBLANES - nb, ATT_DIM), F32)], axis=0)
        gate = _dot_f32(kmean, qt, (((1,), (0,)), ((), ())))
        valid = blk_row < own_blk
        rank = _topk_rank(jnp.where(valid, gate, NEG_INF), 0, nb, blk_row)
        sel = jnp.where(valid, jnp.where(rank < MOBA_TOPK, 1.0, 0.0), 0.0)
        heads.append(dict(chans=chans, sel=sel, qtb=(qt * scale).astype(BF16), kb=k_rows.astype(BF16),
                          vtb=vt_ref[0, chans, :].astype(BF16)))

    tiles = [(hd, i, t0) for i in range(nb) for t0 in range(0, MOBA_BLOCK, tq) for hd in heads]

    def scores(tile):
        hd, i, t0 = tile
        q0 = i * MOBA_BLOCK + t0
        return _dot(hd["kb"][:q0 + tq], hd["qtb"][:, q0:q0 + tq])

    ahead = [scores(t) for t in tiles[:MOBA_LOOKAHEAD]]
    pages_done = 0
    for n, (hd, i, t0) in enumerate(tiles):
        s = ahead.pop(0)
        if n + MOBA_LOOKAHEAD < len(tiles):
            ahead.append(scores(tiles[n + MOBA_LOOKAHEAD]))
        pages_due = ((n + 1) * n_pages) // len(tiles)
        for p in range(pages_done, pages_due):
            score_page(p)
        pages_done = pages_due
        q0 = i * MOBA_BLOCK + t0
        pieces = [jnp.where(hd["sel"][j:j + 1, q0:q0 + tq] > 0.5, s[j * MOBA_BLOCK:(j + 1) * MOBA_BLOCK], NEG_INF)
                  for j in range(i)]
        pieces.append(jnp.where(causal[t0], s[i * MOBA_BLOCK:], NEG_INF))
        sm = pieces[0] if i == 0 else jnp.concatenate(pieces, axis=0)
        p = jnp.exp(sm - jnp.max(sm, axis=0, keepdims=True))
        den = jnp.sum(p, axis=0, keepdims=True)
        ot_ref[0, hd["chans"], q0:q0 + tq] = (_dot(hd["vtb"][:, :q0 + tq], p.astype(BF16)) / den).astype(ot_ref.dtype)

    nblk = n_pages // PAGES_PER_BLOCK
    lane = lax.broadcasted_iota(I32, (ATT_HEADS, LANES), 1)
    gate = jnp.full((ATT_HEADS, LANES), NEG_INF, F32)
    for n in range(nblk):
        g_n = jnp.sum(s_ref[0, n * ATT_HEADS:(n + 1) * ATT_HEADS, :], axis=1, keepdims=True) * (1.0 / MOBA_BLOCK)
        gate = jnp.where(lane == n, g_n, gate)
    rank = _topk_rank(gate, 1, nblk, lane)
    lane_f = lane.astype(F32)
    for j in range(MOBA_TOPK):
        pick = jnp.sum(jnp.where(rank == float(j), jnp.where(lane < nblk, lane_f, 0.0), 0.0), axis=1, keepdims=True)
        sel_ref[0, j] = pick.astype(I32)


def _moba_prompt(page_table, qt3, kt3, vt3, q_dec, cache_kt):
    b, _, l = qt3.shape
    dec, n_pages = page_table.shape
    assert l % MOBA_BLOCK == 0 and l // MOBA_BLOCK <= SUBLANES
    width = MOBA_HEADS_PER_STEP * ATT_DIM
    nj = ATT_WIDTH // width
    assert dec == b * nj, "one decode sequence is scored per prompt attention step"
    nblk = n_pages // PAGES_PER_BLOCK
    assert nblk <= LANES
    spec = pl.BlockSpec((1, width, l), lambda i, j, pt: (i, j, 0))
    per_step = lambda shape: pl.BlockSpec((1,) + shape, lambda i, j, pt: (i * nj + j,) + (0,) * len(shape))
    grid_spec = pltpu.PrefetchScalarGridSpec(
        num_scalar_prefetch=1,
        grid=(b, nj),
        in_specs=[spec, spec, spec, per_step((ATT_HEADS, ATT_DIM)), pl.BlockSpec(memory_space=pl.ANY)],
        out_specs=[spec, per_step((nblk * ATT_HEADS, MOBA_BLOCK)), per_step((MOBA_TOPK, ATT_HEADS, 1))],
        scratch_shapes=[pltpu.VMEM((2, n_pages, ATT_HEADS, ATT_DIM, PAGE_SIZE), F32),
                        pltpu.SemaphoreType.DMA((2,))],
    )
    o_att, scores, sel = pl.pallas_call(
        functools.partial(_moba_prompt_kernel, n_pages=n_pages),
        grid_spec=grid_spec,
        out_shape=[jax.ShapeDtypeStruct((b, ATT_WIDTH, l), BF16),
                   jax.ShapeDtypeStruct((dec, nblk * ATT_HEADS, MOBA_BLOCK), F32),
                   jax.ShapeDtypeStruct((dec, MOBA_TOPK, ATT_HEADS, 1), I32)],
        compiler_params=pltpu.CompilerParams(dimension_semantics=("arbitrary", "arbitrary"),
                                             vmem_limit_bytes=VMEM_LIMIT),
        name="moba_prompt",
    )(page_table, qt3, kt3, vt3, q_dec, cache_kt)
    return o_att, scores, sel.reshape(dec, MOBA_TOPK, ATT_HEADS)


ATT_PAGES = MOBA_TOPK * PAGES_PER_BLOCK


def _moba_attend_kernel(pt_ref, sel_ref, s_ref, q_ref, kn_ref, vn_ref, vc_ref, o_ref, vbuf, sem_ref):
    bi = pl.program_id(0)
    slot = bi % 2
    scale = ATT_DIM ** -0.5

    def copies(b_idx, slot_idx):
        out = []
        for h in range(ATT_HEADS):
            for j in range(MOBA_TOPK):
                blk = sel_ref[b_idx, j, h]
                for p in range(PAGES_PER_BLOCK):
                    page = pt_ref[b_idx, blk * PAGES_PER_BLOCK + p]
                    out.append(pltpu.make_async_copy(vc_ref.at[0, page, h],
                                                     vbuf.at[slot_idx, h, j * PAGES_PER_BLOCK + p],
                                                     sem_ref.at[slot_idx]))
        return out

    @pl.when(bi == 0)
    def _():
        for cp in copies(bi, slot):
            cp.start()

    @pl.when(bi + 1 < pl.num_programs(0))
    def _():
        for cp in copies(bi + 1, 1 - slot):
            cp.start()

    for cp in copies(bi, slot):
        cp.wait()

    eye = (lax.broadcasted_iota(I32, (ATT_DIM, LANES), 0) == lax.broadcasted_iota(I32, (ATT_DIM, LANES), 1))
    s_own_all = jnp.sum(q_ref[0] * kn_ref[0], axis=1, keepdims=True) * scale
    for h in range(ATT_HEADS):
        s_blk = [s_ref[0, pl.ds(sel_ref[bi, j, h] * ATT_HEADS + h, 1), :] * scale for j in range(MOBA_TOPK)]
        s_own = s_own_all[h:h + 1, :]
        mx = s_own
        for s_j in s_blk:
            mx = jnp.maximum(mx, jnp.max(s_j, axis=1, keepdims=True))
        p_blk = [jnp.exp(s_j - mx) for s_j in s_blk]
        p_own = jnp.exp(s_own - mx)
        den = p_own
        acc = jnp.zeros((ATT_DIM, PAGE_SIZE), F32)
        for j, p_j in enumerate(p_blk):
            den = den + jnp.sum(p_j, axis=1, keepdims=True)
            for p in range(PAGES_PER_BLOCK):
                acc = acc + p_j[:, p * PAGE_SIZE:(p + 1) * PAGE_SIZE] * vbuf[slot, h, j * PAGES_PER_BLOCK + p]
        o_col = jnp.sum(acc, axis=1, keepdims=True)
        o_row = jnp.sum(jnp.where(eye, o_col, 0.0), axis=0, keepdims=True)[:, :ATT_DIM]
        o_ref[0, h:h + 1, :] = (o_row + p_own * vn_ref[0, h:h + 1, :]) / den


def _moba_attend(page_table, sel, scores, q3, kn3, vn3, cache_vt):
    b = page_table.shape[0]
    row = pl.BlockSpec((1, ATT_HEADS, ATT_DIM), lambda i, pt, sl: (i, 0, 0))
    grid_spec = pltpu.PrefetchScalarGridSpec(
        num_scalar_prefetch=2,
        grid=(b,),
        in_specs=[pl.BlockSpec((1,) + scores.shape[1:], lambda i, pt, sl: (i, 0, 0)),
                  row, row, row, pl.BlockSpec(memory_space=pl.ANY)],
        out_specs=row,
        scratch_shapes=[pltpu.VMEM((2, ATT_HEADS, ATT_PAGES, ATT_DIM, PAGE_SIZE), F32),
                        pltpu.SemaphoreType.DMA((2,))],
    )
    return pl.pallas_call(
        _moba_attend_kernel,
        grid_spec=grid_spec,
        out_shape=jax.ShapeDtypeStruct((b, ATT_HEADS, ATT_DIM), F32),
        compiler_params=pltpu.CompilerParams(dimension_semantics=("arbitrary",), vmem_limit_bytes=VMEM_LIMIT),
        name="moba_attend",
    )(page_table, sel, scores, q3, kn3, vn3, cache_vt)


FFN_COLS = 256
FFN_LOOKAHEAD = 1


def _post_kernel(x_ref, ohg_ref, oatt_ref, prev_ref, wo_ref, gpm_ref, gpf_ref, gqf_ref, wi_ref, cw_ref, cb_ref,
                 wf_ref, y_ref, conv_ref, gate_ref, act_ref, *, decode):
    li = pl.program_id(1)
    tl = x_ref.shape[1]
    d_ff = wf_ref.shape[0]
    if decode:
        att = _dot(oatt_ref[0].astype(BF16), wo_ref[HG_WIDTH:, :])
    else:
        att = lax.dot_general(oatt_ref[0].astype(BF16), wo_ref[HG_WIDTH:, :], _TN, preferred_element_type=F32)
    mix = _dot(ohg_ref[0].astype(BF16), wo_ref[:HG_WIDTH, :]) + att
    x1 = x_ref[0] + _rms(mix, gpm_ref[...])
    h2 = _rms(x1, gpf_ref[...]).astype(BF16)

    if not decode:
        @pl.when(li == 0)
        def _():
            gate_ref[0:SUBLANES, :] = jnp.zeros((SUBLANES, d_ff), F32)
            gate_ref[SUBLANES - 2:SUBLANES - 1, :] = prev_ref[0, 0]
            gate_ref[SUBLANES - 1:SUBLANES, :] = prev_ref[0, 1]

    def project(c0):
        return (_dot(h2, wi_ref[:, c0:c0 + FFN_COLS]), _dot(h2, wi_ref[:, d_ff + c0:d_ff + c0 + FFN_COLS]))

    chunks = list(range(0, d_ff, FFN_COLS))
    ahead = [project(c0) for c0 in chunks[:FFN_LOOKAHEAD]]
    for n, c0 in enumerate(chunks):
        cols = slice(c0, c0 + FFN_COLS)
        gate, up = ahead.pop(0)
        if n + FFN_LOOKAHEAD < len(chunks):
            ahead.append(project(chunks[n + FFN_LOOKAHEAD]))
        w0 = cw_ref[0:1, cols]
        w1 = cw_ref[1:2, cols]
        w2 = cw_ref[2:3, cols]
        if decode:
            conv = prev_ref[0, 0, :, cols] * w0 + prev_ref[0, 1, :, cols] * w1 + gate * w2 + cb_ref[:, cols]
            conv_ref[0, 0, :, cols] = prev_ref[0, 1, :, cols]
            conv_ref[0, 1, :, cols] = gate
        else:
            gate_ref[SUBLANES:SUBLANES + tl, cols] = gate
            conv = (gate_ref[SUBLANES - 2:SUBLANES - 2 + tl, cols] * w0
                    + gate_ref[SUBLANES - 1:SUBLANES - 1 + tl, cols] * w1 + gate * w2 + cb_ref[:, cols])
        act_ref[:, cols] = (jax.nn.gelu(conv, approximate=True) * up).astype(BF16)
    y_ref[0] = x1 + _rms(_dot(act_ref[...], wf_ref[...]), gqf_ref[...])

    if not decode:
        last = gate_ref[tl:tl + SUBLANES, :]
        gate_ref[0:SUBLANES, :] = last

        @pl.when(li == pl.num_programs(1) - 1)
        def _():
            conv_ref[0, 0] = last[SUBLANES - 2:SUBLANES - 1, :]
            conv_ref[0, 1] = last[SUBLANES - 1:SUBLANES, :]


def _post(x3, ohg3, oatt3, prev, wo, gpm, gpf, gqf, wi, cw, cb, wf, tl, decode):
    b, l, d = x3.shape
    d_ff = wf.shape[0]
    row = lambda width: pl.BlockSpec((1, tl, width), lambda i, j: (i, j, 0))
    const = lambda shape: pl.BlockSpec(shape, lambda i, j: (0,) * len(shape), pipeline_mode=pl.Buffered(1))
    if decode:
        state = pl.BlockSpec((1, CONV_WIDTH - 1, tl, d_ff), lambda i, j: (i, 0, j, 0))
        att = row(ATT_WIDTH)
    else:
        state = pl.BlockSpec((1, CONV_WIDTH - 1, 1, d_ff), lambda i, j: (i, 0, 0, 0))
        att = pl.BlockSpec((1, ATT_WIDTH, tl), lambda i, j: (i, 0, j))
    return pl.pallas_call(
        functools.partial(_post_kernel, decode=decode),
        grid=(b, l // tl),
        in_specs=[row(d), row(HG_WIDTH), att, state,
                  const(wo.shape), const((1, d)), const((1, d)), const((1, d)),
                  const(wi.shape), const(cw.shape), const((1, d_ff)), const(wf.shape)],
        out_specs=[row(d), state],
        out_shape=[jax.ShapeDtypeStruct((b, l, d), F32),
                   jax.ShapeDtypeStruct(prev.shape, F32)],
        scratch_shapes=[pltpu.VMEM((tl + SUBLANES, d_ff), F32), pltpu.VMEM((tl, d_ff), BF16)],
        compiler_params=pltpu.CompilerParams(dimension_semantics=("arbitrary", "arbitrary"),
                                             vmem_limit_bytes=VMEM_LIMIT),
        name="post_decode" if decode else "post_prompt",
    )(x3, ohg3, oatt3, prev, wo, gpm, gpf, gqf, wi, cw, cb, wf)


INPROJ_ROWS = 512
POST_ROWS = 512
HGRN_CHUNK = 256


def kernel(x_prompt, x_sample, cache_k, cache_v, state_hgrn, state_conv, page_table, w_in, w_out, hg_lb_logits,
           hg_norm, g_pre_mix, g_post_mix, g_pre_ffn, g_post_ffn, w_ffn_in, conv_w, conv_b, w_ffn_out):
    depth = w_in.shape[0]
    assert depth == 1, "single-layer step"
    bp, seq, d = x_prompt.shape
    bs, t_dec, _ = x_sample.shape
    assert t_dec == 1
    n_pages = page_table.shape[1]
    past = n_pages * PAGE_SIZE
    lb_all = jnp.cumsum(jax.nn.softmax(hg_lb_logits.astype(F32), axis=0), axis=0)

    layer = 0
    lb = lb_all[layer][None, :]
    ng = hg_norm[layer][None, :]
    w_in_b = w_in[layer].astype(BF16)
    w_out_b = w_out[layer].astype(BF16)
    w_fi_b = w_ffn_in[layer].astype(BF16)
    w_fo_b = w_ffn_out[layer].astype(BF16)
    gains = [g[layer][None, :] for g in (g_pre_mix, g_post_mix, g_pre_ffn, g_post_ffn)]
    cw = conv_w[layer]
    cb = conv_b[layer][None, :]
    post_w = (w_out_b, gains[1], gains[2], gains[3], w_fi_b, cw, cb, w_fo_b)

    tabs_s = _rope_tables(jnp.full((bs,), past, dtype=I32))
    zh_s, q_s, k_s, v_s = _inproj(x_sample.reshape(bs, d), gains[0], w_in_b, tabs_s, bs)
    heads = lambda a: a.reshape(bs, ATT_HEADS, ATT_DIM)
    cache_kt = jnp.transpose(cache_k, (0, 1, 3, 4, 2))
    cache_vt = jnp.transpose(cache_v, (0, 1, 3, 4, 2))

    tm = min(INPROJ_ROWS, seq)
    cos_t, sin_t = _rope_tables_t(jnp.arange(seq, dtype=I32))
    w_h = w_in_b[:, :4 * HG_WIDTH]
    w_at = w_in_b[:, 4 * HG_WIDTH:].T
    qt, kt, vt, o_hg, s_p = _prompt_mix(x_prompt, gains[0], w_h, w_at, cos_t, sin_t, lb, ng, tm,
                                        min(HGRN_CHUNK, seq))
    o_att, scores, sel = _moba_prompt(page_table, qt, kt, vt, heads(q_s), cache_kt)
    d_ff = w_ffn_out.shape[1]
    zero_prev = jnp.zeros((bp, CONV_WIDTH - 1, 1, d_ff), F32)
    y_p, conv_p = _post(x_prompt, o_hg, o_att, zero_prev, *post_w, tl=min(POST_ROWS, seq), decode=False)
    conv_p = conv_p.reshape(bp, CONV_WIDTH - 1, d_ff)

    o_hg_s, s_s = _hgrn_step(zh_s.reshape(bs, 1, 4 * HG_WIDTH), lb, ng, state_hgrn[layer])
    o_att_s = _moba_attend(page_table, sel, scores, heads(q_s), heads(k_s), heads(v_s), cache_vt)
    rows3 = lambda a: a.reshape(1, bs, a.shape[-1])
    prev_s = jnp.transpose(state_conv[layer], (1, 0, 2))[None]
    y_s, conv_s = _post(rows3(x_sample), rows3(o_hg_s), o_att_s.reshape(1, bs, ATT_WIDTH), prev_s, *post_w,
                        tl=bs, decode=True)
    y_s = y_s.reshape(bs, 1, d)
    conv_s = jnp.transpose(conv_s[0], (1, 0, 2))

    kv_p = lambda a: jnp.transpose(a.reshape(1, bp, ATT_HEADS, ATT_DIM, seq), (0, 1, 4, 2, 3))
    kv_s = lambda a: a.reshape(1, bs, 1, ATT_HEADS, ATT_DIM)
    return (y_p, y_s, kv_p(kt), kv_p(vt), s_p[None], conv_p[None],
            kv_s(k_s), kv_s(v_s), s_s[None], conv_s[None])
```
